```python
import jax, jax.numpy as jnp
from jax import lax
import numpy as np

D_MODEL = 1024
BATCH = 32
SEQ = 2048
DEPTH = 4
DEC_BATCH = 4
DEC_SEQ = 4096
PAST_LEN = 128

N_MEM = 256
D_A = D_MODEL // 2
HEAD_A = 64
H_A = D_A // HEAD_A
LORA_W = 64
LORA_A = 64
LORA_G = 128
N_DIR = 2
D_B = D_MODEL // 2
CONV_K = 31
D_C = D_MODEL // 2
G_C = 4
GW_C = D_C // G_C
N_BRANCH = 3
N_A_IN = 3 * D_A + N_DIR * LORA_W + N_DIR * LORA_A + LORA_G
N_IN = N_A_IN + 2 * D_B + D_C + N_BRANCH * D_MODEL
H_X = 4
HD_X = D_MODEL // H_X
D_FF = 4 * D_MODEL

EPS_RMS = 1e-6
EPS_LN = 1e-5
EPS_GN = 64e-5

kernel_name = "hybrid_rwkv7_conformer_fnet_encoder"


def _f32(t):
    return t.astype(jnp.float32)


def rms_norm(x, g):
    x32 = _f32(x)
    y = x32 * lax.rsqrt(jnp.mean(x32 * x32, axis=-1, keepdims=True) + EPS_RMS)
    return (y * _f32(g)).astype(x.dtype)


def centred_shift(z, mu):
    zero = jnp.zeros_like(z[:, :1])
    prev = jnp.concatenate([zero, z[:, :-1]], axis=1)
    nxt = jnp.concatenate([z[:, 1:], zero], axis=1)
    return z + mu[0] * (prev - z) + mu[1] * (nxt - z)


def wkv_scan(r, decay, k, v, a_vec, b_vec, reverse):
    bsz, nh, n = r.shape[1:]

    def step(st, inp):
        r_t, w_t, k_t, v_t, a_t, b_t = inp
        sa = jnp.einsum('bhvk,bhk->bhv', st, a_t)
        st = st * w_t[:, :, None, :] + sa[..., None] * b_t[:, :, None, :] + v_t[..., None] * k_t[:, :, None, :]
        return st, jnp.einsum('bhvk,bhk->bhv', st, r_t)

    s0 = jnp.zeros((bsz, nh, n, n), jnp.float32)
    _, y = lax.scan(step, s0, (r, decay, k, v, a_vec, b_vec), reverse=reverse)
    return y


def rwkv_branch(za, mu, w0, w_up, a0, a_up, g_up, k_k, k_a, r_k, lnx_g, lnx_b, w_a_out):
    bsz, s, _ = za.shape
    dt = za.dtype
    za = _f32(centred_shift(za, mu))
    r = za[..., :D_A]
    k = za[..., D_A:2 * D_A]
    v = za[..., 2 * D_A:3 * D_A]
    o = 3 * D_A
    dw = za[..., o:o + N_DIR * LORA_W].reshape(bsz, s, N_DIR, LORA_W)
    o += N_DIR * LORA_W
    da = za[..., o:o + N_DIR * LORA_A].reshape(bsz, s, N_DIR, LORA_A)
    o += N_DIR * LORA_A
    dg = za[..., o:o + LORA_G]
    wl = _f32(w0) + jnp.einsum('bsdr,drc->bsdc', jnp.tanh(dw), _f32(w_up))
    decay = jnp.exp(-jnp.exp(-jax.nn.softplus(-wl) - 0.5))
    a = jax.nn.sigmoid(_f32(a0) + jnp.einsum('bsdr,drc->bsdc', da, _f32(a_up)))
    g = jax.nn.sigmoid(dg) @ _f32(g_up)
    heads = lambda t: t.reshape(t.shape[:-1] + (H_A, HEAD_A))
    kk = heads(k * _f32(k_k))
    kk = kk / jnp.maximum(jnp.sqrt(jnp.sum(kk * kk, axis=-1, keepdims=True)), 1e-12)
    kd = heads(k[:, :, None, :] * (1.0 + (a - 1.0) * _f32(k_a)))
    rh, vh = heads(r), heads(v)
    dech = heads(decay)
    bh = kk[:, :, None] * heads(a)
    tm = lambda t: jnp.swapaxes(t, 0, 1)
    y = (wkv_scan(tm(rh), tm(dech[:, :, 0]), tm(kd[:, :, 0]), tm(vh), tm(-kk), tm(bh[:, :, 0]), False)
         + wkv_scan(tm(rh), tm(dech[:, :, 1]), tm(kd[:, :, 1]), tm(vh), tm(-kk), tm(bh[:, :, 1]), True))
    y = tm(y)
    m = jnp.mean(y, axis=-1, keepdims=True)
    var = jnp.mean(jnp.square(y - m), axis=-1, keepdims=True)
    yn = ((y - m) * lax.rsqrt(var + EPS_GN)).reshape(bsz, s, D_A) * _f32(lnx_g) + _f32(lnx_b)
    bonus = jnp.sum(jnp.sum(rh[:, :, None] * kd * _f32(r_k), axis=-1, keepdims=True), axis=2) * vh
    y = (yn + bonus.reshape(bsz, s, D_A)) * g
    return y.astype(dt) @ w_a_out


def conv_branch(zb, conv_w, conv_b, ln_g, ln_b, w_b_out):
    u = zb[..., :D_B] * jax.nn.sigmoid(zb[..., D_B:])
    c = lax.conv_general_dilated(u, conv_w[:, None, :].astype(u.dtype), (1,),
                                 [(CONV_K // 2, CONV_K // 2)],
                                 dimension_numbers=('NWC', 'WIO', 'NWC'),
                                 feature_group_count=D_B) + conv_b
    c32 = _f32(c)
    m = jnp.mean(c32, axis=-1, keepdims=True)
    var = jnp.mean(jnp.square(c32 - m), axis=-1, keepdims=True)
    cn = (c32 - m) * lax.rsqrt(var + EPS_LN) * _f32(ln_g) + _f32(ln_b)
    return jax.nn.silu(cn).astype(zb.dtype) @ w_b_out


def fourier_branch(zc, w_c_out):
    bsz, s, _ = zc.shape
    f = _f32(zc).reshape(bsz, s, G_C, GW_C)
    f = jnp.fft.fft2(f, axes=(1, 3), norm='ortho').real
    return f.reshape(bsz, s, D_C).astype(zc.dtype) @ w_c_out


def cross_attn(h, mem_n, wq, wk, wv, wo):
    bsz, s, _ = h.shape
    q = (h @ wq).reshape(bsz, s, H_X, HD_X)
    k = (mem_n @ wk).reshape(bsz, -1, H_X, HD_X)
    v = (mem_n @ wv).reshape(bsz, -1, H_X, HD_X)
    sc = _f32(jnp.einsum('bqhd,bkhd->bhqk', q, k)) * (HD_X ** -0.5)
    p = jax.nn.softmax(sc, axis=-1).astype(v.dtype)
    o = jnp.einsum('bhqk,bkhd->bqhd', p, v).reshape(bsz, s, D_MODEL)
    return o @ wo


def trunk(x, mem, P):
    bsz, s, _ = x.shape
    for l in range(DEPTH):
        h = rms_norm(x, P['norm_mix_g'][l])
        z = h @ P['w_in'][l]
        za = z[..., :N_A_IN]
        zb = z[..., N_A_IN:N_A_IN + 2 * D_B]
        zc = z[..., N_A_IN + 2 * D_B:N_A_IN + 2 * D_B + D_C]
        zg = z[..., N_A_IN + 2 * D_B + D_C:]
        ya = rwkv_branch(za, P['shift_mu'][l], P['w0'][l], P['w_up'][l], P['a0'][l], P['a_up'][l],
                         P['g_up'][l], P['k_k'][l], P['k_a'][l], P['r_k'][l], P['lnx_g'][l],
                         P['lnx_b'][l], P['w_a_out'][l])
        yb = conv_branch(zb, P['conv_w'][l], P['conv_b'][l], P['conv_ln_g'][l], P['conv_ln_b'][l],
                         P['w_b_out'][l])
        yc = fourier_branch(zc, P['w_c_out'][l])
        gates = jax.nn.sigmoid(zg.reshape(bsz, s, N_BRANCH, D_MODEL) + P['gate_b'][l])
        merged = gates[:, :, 0] * ya + gates[:, :, 1] * yb + gates[:, :, 2] * yc
        x = x + merged @ P['w_out'][l]
        h = rms_norm(x, P['norm_x_g'][l])
        mn = rms_norm(mem, P['norm_mem_g'][l])
        x = x + cross_attn(h, mn, P['wq'][l], P['wk'][l], P['wv'][l], P['wo'][l])
        h = rms_norm(x, P['norm_mlp_g'][l])
        x = x + jnp.square(jax.nn.relu(h @ P['w_mlp1'][l])) @ P['w_mlp2'][l]
    return rms_norm(x, P['final_norm_g'])


def setup_inputs(seed: int = 0) -> dict:
    key = jax.random.key(seed)
    ks = iter(jax.random.split(key, 48))
    L = DEPTH
    nrm = lambda shape, sc: jax.random.normal(next(ks), shape, jnp.float32) * sc
    gain = lambda shape: 1.0 + nrm(shape, 0.02)
    d = {}
    d['x_prompt'] = nrm((BATCH, SEQ, D_MODEL), 1.0)
    d['x_sample'] = nrm((DEC_BATCH, DEC_SEQ, D_MODEL), 1.0)
    d['mem_prompt'] = nrm((BATCH, N_MEM, D_MODEL), 1.0)
    d['mem_sample'] = nrm((DEC_BATCH, N_MEM, D_MODEL), 1.0)
    d['norm_mix_g'] = gain((L, D_MODEL))
    d['w_in'] = nrm((L, D_MODEL, N_IN), D_MODEL ** -0.5)
    d['shift_mu'] = jax.random.uniform(next(ks), (L, 2, N_A_IN), jnp.float32, 0.05, 0.45)
    d['w0'] = -1.0 + nrm((L, N_DIR, D_A), 0.5)
    d['w_up'] = nrm((L, N_DIR, LORA_W, D_A), 0.5 * LORA_W ** -0.5)
    d['a0'] = nrm((L, N_DIR, D_A), 0.3)
    d['a_up'] = nrm((L, N_DIR, LORA_A, D_A), 0.5 * LORA_A ** -0.5)
    d['g_up'] = nrm((L, LORA_G, D_A), LORA_G ** -0.5)
    d['k_k'] = 0.85 + nrm((L, D_A), 0.02)
    d['k_a'] = gain((L, D_A))
    d['r_k'] = nrm((L, H_A, HEAD_A), 0.1)
    d['lnx_g'] = gain((L, D_A))
    d['lnx_b'] = nrm((L, D_A), 0.02)
    d['w_a_out'] = nrm((L, D_A, D_MODEL), D_A ** -0.5)
    d['conv_w'] = nrm((L, CONV_K, D_B), CONV_K ** -0.5)
    d['conv_b'] = nrm((L, D_B), 0.02)
    d['conv_ln_g'] = gain((L, D_B))
    d['conv_ln_b'] = nrm((L, D_B), 0.02)
    d['w_b_out'] = nrm((L, D_B, D_MODEL), D_B ** -0.5)
    d['w_c_out'] = nrm((L, D_C, D_MODEL), D_C ** -0.5)
    d['gate_b'] = nrm((L, N_BRANCH, D_MODEL), 0.02)
    d['w_out'] = nrm((L, D_MODEL, D_MODEL), D_MODEL ** -0.5)
    d['norm_x_g'] = gain((L, D_MODEL))
    d['norm_mem_g'] = gain((L, D_MODEL))
    d['wq'] = nrm((L, D_MODEL, D_MODEL), D_MODEL ** -0.5)
    d['wk'] = nrm((L, D_MODEL, D_MODEL), D_MODEL ** -0.5)
    d['wv'] = nrm((L, D_MODEL, D_MODEL), D_MODEL ** -0.5)
    d['wo'] = nrm((L, D_MODEL, D_MODEL), D_MODEL ** -0.5)
    d['norm_mlp_g'] = gain((L, D_MODEL))
    d['w_mlp1'] = nrm((L, D_MODEL, D_FF), D_MODEL ** -0.5)
    d['w_mlp2'] = nrm((L, D_FF, D_MODEL), D_FF ** -0.5)
    d['final_norm_g'] = gain((D_MODEL,))
    return d


def reference(x_prompt, x_sample, mem_prompt, mem_sample, norm_mix_g, w_in, shift_mu, w0, w_up,
              a0, a_up, g_up, k_k, k_a, r_k, lnx_g, lnx_b, w_a_out, conv_w, conv_b, conv_ln_g,
              conv_ln_b, w_b_out, w_c_out, gate_b, w_out, norm_x_g, norm_mem_g, wq, wk, wv, wo,
              norm_mlp_g, w_mlp1, w_mlp2, final_norm_g):
    P = dict(norm_mix_g=norm_mix_g, w_in=w_in, shift_mu=shift_mu, w0=w0, w_up=w_up, a0=a0,
             a_up=a_up, g_up=g_up, k_k=k_k, k_a=k_a, r_k=r_k, lnx_g=lnx_g, lnx_b=lnx_b,
             w_a_out=w_a_out, conv_w=conv_w, conv_b=conv_b, conv_ln_g=conv_ln_g,
             conv_ln_b=conv_ln_b, w_b_out=w_b_out, w_c_out=w_c_out, gate_b=gate_b, w_out=w_out,
             norm_x_g=norm_x_g, norm_mem_g=norm_mem_g, wq=wq, wk=wk, wv=wv, wo=wo,
             norm_mlp_g=norm_mlp_g, w_mlp1=w_mlp1, w_mlp2=w_mlp2, final_norm_g=final_norm_g)
    y_prompt = trunk(x_prompt, mem_prompt, P)
    y_sample = trunk(x_sample, mem_sample, P)
    return (y_prompt, y_sample)
```

```python
import functools
import math

import jax
import jax.numpy as jnp
from jax import lax
from jax.experimental import pallas as pl
from jax.experimental.pallas import tpu as pltpu

F32 = jnp.float32
BF16 = jnp.bfloat16

D_MODEL = 1024
DEPTH = 4
N_MEM = 256
D_A = 512
HEAD_A = 64
LORA = 64
LORA_G = 128
N_A_IN = 3 * D_A + 4 * LORA + LORA_G
D_B = 512
CONV_K = 31
D_C = 512
GW_C = 128
H_X = 4
HD_X = D_MODEL // H_X
D_FF = 4 * D_MODEL

EPS_RMS = 1e-6
EPS_LN = 1e-5
EPS_GN = 64e-5

VMEM_LIMIT_BYTES = 56 * 1024 * 1024

WKV_CHUNK = 64
WKV_HALF = 256
CONV_HALO = 16
SHIFT_HALO = 8


def _params(*sem):
    return pltpu.CompilerParams(dimension_semantics=sem, vmem_limit_bytes=VMEM_LIMIT_BYTES)


def _dot(a, b):
    return jnp.dot(a.astype(BF16), b.astype(BF16), preferred_element_type=F32)


def _dot_nt(a, b):
    return lax.dot_general(a.astype(BF16), b.astype(BF16), (((1,), (1,)), ((), ())),
                           preferred_element_type=F32)


def _split_dot(x, m):
    xh = x.astype(BF16)
    xl = (x - xh.astype(F32)).astype(BF16)
    return jnp.dot(xh, m, preferred_element_type=F32) + jnp.dot(xl, m, preferred_element_type=F32)


def _rms(x, g):
    return x * lax.rsqrt(jnp.mean(x * x, axis=-1, keepdims=True) + EPS_RMS) * g


def _in_proj_kernel(x_ref, g_ref, w_ref, za_ref, zb_ref, zc_ref, zg_ref):
    hn = _rms(x_ref[...], g_ref[...]).astype(BF16)
    off = 0
    for o_ref in (za_ref, zb_ref, zc_ref, zg_ref):
        n = o_ref.shape[-1]
        o_ref[...] = jnp.dot(hn, w_ref[:, off:off + n], preferred_element_type=F32)
        off += n


def _in_proj(x2, g, w_in, tm=256):
    t = x2.shape[0]
    widths = (N_A_IN, 2 * D_B, D_C, 3 * D_MODEL)
    return pl.pallas_call(
        _in_proj_kernel,
        grid=(t // tm,),
        in_specs=[pl.BlockSpec((tm, D_MODEL), lambda i: (i, 0)),
                  pl.BlockSpec((1, D_MODEL), lambda i: (0, 0)),
                  pl.BlockSpec(w_in.shape, lambda i: (0, 0))],
        out_specs=[pl.BlockSpec((tm, n), lambda i: (i, 0)) for n in widths],
        out_shape=[jax.ShapeDtypeStruct((t, n), F32) for n in widths],
        compiler_params=_params("parallel"),
        name="in_proj",
    )(x2, g, w_in)


def _rwkv_prep_kernel(z_ref, zp_ref, zn_ref, mu_ref, w0_ref, wup_ref, a0_ref, aup_ref, gup_ref,
                      kk_ref, ka_ref, rk_ref, bd_ref,
                      r_ref, v_ref, na_ref, g_ref, bon_ref, lw_ref, kd_ref, b_ref):
    i = pl.program_id(1)
    n = pl.num_programs(1)
    z = z_ref[0]
    ts = z.shape[0]
    row = lax.broadcasted_iota(jnp.int32, z.shape, 0)
    prev_row = jnp.where(i > 0, zp_ref[0, SHIFT_HALO - 1:SHIFT_HALO, :], 0.0)
    next_row = jnp.where(i < n - 1, zn_ref[0, 0:1, :], 0.0)
    zprev = jnp.where(row == 0, prev_row, pltpu.roll(z, 1, axis=0))
    znext = jnp.where(row == ts - 1, next_row, pltpu.roll(z, ts - 1, axis=0))
    zs = z + mu_ref[0:1, :] * (zprev - z) + mu_ref[1:2, :] * (znext - z)

    r = zs[:, 0:D_A]
    k = zs[:, D_A:2 * D_A]
    v = zs[:, 2 * D_A:3 * D_A]
    o = 3 * D_A
    tdw = jnp.tanh(zs[:, o:o + 2 * LORA]).astype(BF16)
    dab = zs[:, o + 2 * LORA:o + 4 * LORA].astype(BF16)
    dg = zs[:, o + 4 * LORA:o + 4 * LORA + LORA_G]
    bd = bd_ref[...]

    g_ref[0] = _dot(jax.nn.sigmoid(dg), gup_ref[...])
    kk0 = k * kk_ref[...]
    nrm = jnp.sqrt(_split_dot(kk0 * kk0, bd))
    kk = kk0 / jnp.maximum(nrm, 1e-12)
    r_ref[0] = r
    v_ref[0] = v
    na_ref[0] = -kk
    ksum = jnp.zeros_like(k)
    for d in range(2):
        wl = w0_ref[d:d + 1, :] + jnp.dot(tdw, wup_ref[d], preferred_element_type=F32)
        lw_ref[0, d] = (-math.exp(-0.5)) * jax.nn.sigmoid(wl)
        a = jax.nn.sigmoid(a0_ref[d:d + 1, :] + jnp.dot(dab, aup_ref[d], preferred_element_type=F32))
        kd = k * (1.0 + (a - 1.0) * ka_ref[...])
        kd_ref[0, d] = kd
        b_ref[0, d] = kk * a
        ksum = ksum + kd
    bon_ref[0] = _split_dot(r * rk_ref[...] * ksum, bd) * v


def _rwkv_prep(za, lw, ts=256):
    b, s, _ = za.shape
    nh = ts // SHIFT_HALO
    last = s // SHIFT_HALO - 1
    full = lambda a: pl.BlockSpec(a.shape, lambda bi, i: (0,) * a.ndim)
    consts = (lw["mu"], lw["w0"], lw["wup"], lw["a0"], lw["aup"], lw["gup"], lw["k_k"], lw["k_a"],
              lw["r_k"], lw["bd_ones"])
    tok = pl.BlockSpec((1, ts, D_A), lambda bi, i: (bi, i, 0))
    dirs = pl.BlockSpec((1, 2, ts, D_A), lambda bi, i: (bi, 0, i, 0))
    tok_shape = jax.ShapeDtypeStruct((b, s, D_A), F32)
    dir_shape = jax.ShapeDtypeStruct((b, 2, s, D_A), F32)
    return pl.pallas_call(
        _rwkv_prep_kernel,
        grid=(b, s // ts),
        in_specs=[pl.BlockSpec((1, ts, N_A_IN), lambda bi, i: (bi, i, 0)),
                  pl.BlockSpec((1, SHIFT_HALO, N_A_IN), lambda bi, i: (bi, jnp.maximum(i * nh - 1, 0), 0)),
                  pl.BlockSpec((1, SHIFT_HALO, N_A_IN), lambda bi, i: (bi, jnp.minimum((i + 1) * nh, last), 0)),
                  ] + [full(c) for c in consts],
        out_specs=[tok] * 5 + [dirs] * 3,
        out_shape=[tok_shape] * 5 + [dir_shape] * 3,
        compiler_params=_params("parallel", "parallel"),
        name="rwkv_prep",
    )(za, za, za, *consts)


def _bd_expand(x, bdmask):
    return jnp.where(bdmask, jnp.concatenate([x] * (WKV_HALF // WKV_CHUNK), axis=0), 0.0)


def _wkv_chunk_half(r, a, b, k, v, lw, sbd, sgn, sel_row):
    C, HW = WKV_CHUNK, WKV_HALF
    row = lax.broadcasted_iota(jnp.int32, (C, HW), 0)
    lane_i = lax.broadcasted_iota(jnp.int32, (C, HW), 1) & (C - 1)
    rr = lax.broadcasted_iota(jnp.int32, (HW, HW), 0)
    cc = lax.broadcasted_iota(jnp.int32, (HW, HW), 1)
    bdmask = (rr >> 6) == (cc >> 6)
    diag = rr == cc
    ti = lax.broadcasted_iota(jnp.int32, (C, C), 0)
    tj = lax.broadcasted_iota(jnp.int32, (C, C), 1)
    tri = jnp.where((ti - tj) * sgn >= 0, 1.0, 0.0).astype(BF16)
    dt = (row - lane_i) * sgn
    strict = dt > 0
    incl = dt >= 0

    l1 = lw.astype(BF16)
    r1 = lw - l1.astype(F32)
    l2 = r1.astype(BF16)
    l3 = (r1 - l2.astype(F32)).astype(BF16)
    cum = (jnp.dot(tri, l1, preferred_element_type=F32) + jnp.dot(tri, l2, preferred_element_type=F32)
           + jnp.dot(tri, l3, preferred_element_type=F32))
    tot = jnp.sum(jnp.where(row == sel_row, cum, 0.0), axis=0, keepdims=True)
    e_inv = jnp.exp(-cum)
    e_rem = jnp.exp(tot - cum)
    at = a * jnp.exp(cum - lw)
    rt = r * jnp.exp(cum)
    bt = b * e_inv
    kt = k * e_inv
    bh = b * e_rem
    kh = k * e_rem

    gram = _dot_nt(jnp.concatenate([at, rt], axis=0),
                   jnp.concatenate([_bd_expand(bt, bdmask), _bd_expand(kt, bdmask)], axis=0))
    l_ab = jnp.where(strict, gram[:C, :HW], 0.0)
    l_ak = jnp.where(strict, gram[:C, HW:], 0.0)
    m_rb = jnp.where(incl, gram[C:, :HW], 0.0)
    m_rk = jnp.where(incl, gram[C:, HW:], 0.0)

    t_inv = jnp.where(dt == 0, 1.0, 0.0) + l_ab
    lp = l_ab
    for _ in range(int(math.log2(C)) - 1):
        lp = _dot(lp, _bd_expand(lp, bdmask))
        t_inv = t_inv + _dot(t_inv, _bd_expand(lp, bdmask))

    vbd = _bd_expand(v, bdmask)
    lakv = _dot(l_ak, vbd)
    pq = _dot(t_inv, jnp.concatenate([_bd_expand(at, bdmask), _bd_expand(lakv, bdmask)], axis=1))
    pm = pq[:, :HW]
    qm = pq[:, HW:]
    mpq = _dot(m_rb, jnp.concatenate([_bd_expand(pm, bdmask), _bd_expand(qm, bdmask)], axis=1))
    rp = rt + mpq[:, :HW]
    yq = mpq[:, HW:] + _dot(m_rk, vbd)

    lhs_t = jnp.transpose(jnp.concatenate([bh, kh], axis=0))
    rhs = jnp.concatenate([jnp.concatenate([pm, qm], axis=1),
                           jnp.concatenate([jnp.zeros_like(pm), v], axis=1)], axis=0)
    gh = _dot(lhs_t, rhs)
    wc_col = jnp.sum(jnp.where(diag, jnp.exp(tot), 0.0), axis=1, keepdims=True)
    gbd = jnp.where(bdmask, gh[:, :HW], 0.0) + jnp.where(diag, wc_col, 0.0)
    hbd = jnp.where(bdmask, gh[:, HW:], 0.0)

    out = _dot(jnp.concatenate([gbd, rp], axis=0), sbd)
    return out[HW:] + yq, out[:HW] + hbd


def _wkv_kernel(r_ref, v_ref, na_ref, lw_ref, kd_ref, b_ref, y_ref, s_ref):
    d = pl.program_id(1)
    c = pl.program_id(2)

    @pl.when(c == 0)
    def _():
        s_ref[...] = jnp.zeros_like(s_ref)

    sgn = 1 - 2 * d
    sel_row = (WKV_CHUNK - 1) * (1 - d)
    for h in range(D_A // WKV_HALF):
        sl = slice(h * WKV_HALF, (h + 1) * WKV_HALF)
        y, s_new = _wkv_chunk_half(r_ref[0, :, sl], na_ref[0, :, sl], b_ref[0, 0, :, sl], kd_ref[0, 0, :, sl],
                                   v_ref[0, :, sl], lw_ref[0, 0, :, sl], s_ref[h], sgn, sel_row)
        y_ref[0, 0, :, sl] = y
        s_ref[h] = s_new


def _wkv(r, v, na, lw, kd, bb):
    b, s, _ = r.shape
    nc = s // WKV_CHUNK
    cidx = lambda d, c: c + d * (nc - 1 - 2 * c)
    tok = pl.BlockSpec((1, WKV_CHUNK, D_A), lambda bi, d, c: (bi, cidx(d, c), 0))
    dirs = pl.BlockSpec((1, 1, WKV_CHUNK, D_A), lambda bi, d, c: (bi, d, cidx(d, c), 0))
    return pl.pallas_call(
        _wkv_kernel,
        grid=(b, 2, nc),
        in_specs=[tok] * 3 + [dirs] * 3,
        out_specs=dirs,
        out_shape=jax.ShapeDtypeStruct((b, 2, s, D_A), F32),
        scratch_shapes=[pltpu.VMEM((D_A // WKV_HALF, WKV_HALF, WKV_HALF), F32)],
        compiler_params=_params("parallel", "parallel", "arbitrary"),
        name="wkv_scan",
    )(r, v, na, lw, kd, bb)


def _conv_kernel(z_ref, zp_ref, zn_ref, w_ref, b_ref, c_ref, u_ref, *, sub):
    i = pl.program_id(1)
    n = pl.num_programs(1)
    ts = z_ref.shape[1]
    glu = lambda t: t[:, :D_B] * jax.nn.sigmoid(t[:, D_B:])
    u_ref[0:CONV_HALO, :] = jnp.where(i > 0, glu(zp_ref[0]), 0.0)
    u_ref[CONV_HALO:CONV_HALO + ts, :] = glu(z_ref[0])
    u_ref[CONV_HALO + ts:2 * CONV_HALO + ts, :] = jnp.where(i < n - 1, glu(zn_ref[0]), 0.0)
    first = CONV_HALO - CONV_K // 2
    for t0 in range(0, ts, sub):
        acc = jnp.zeros((sub, D_B), F32) + b_ref[...]
        for j in range(CONV_K):
            acc = acc + w_ref[j:j + 1, :] * u_ref[t0 + first + j:t0 + first + j + sub, :]
        c_ref[0, t0:t0 + sub, :] = acc


def _conv(zb, lw, ts=256, sub=32):
    b, s, _ = zb.shape
    nh = ts // CONV_HALO
    last = s // CONV_HALO - 1
    return pl.pallas_call(
        functools.partial(_conv_kernel, sub=sub),
        grid=(b, s // ts),
        in_specs=[pl.BlockSpec((1, ts, 2 * D_B), lambda bi, i: (bi, i, 0)),
                  pl.BlockSpec((1, CONV_HALO, 2 * D_B), lambda bi, i: (bi, jnp.maximum(i * nh - 1, 0), 0)),
                  pl.BlockSpec((1, CONV_HALO, 2 * D_B), lambda bi, i: (bi, jnp.minimum((i + 1) * nh, last), 0)),
                  pl.BlockSpec((CONV_K + 1, D_B), lambda bi, i: (0, 0)),
                  pl.BlockSpec((1, D_B), lambda bi, i: (0, 0))],
        out_specs=pl.BlockSpec((1, ts, D_B), lambda bi, i: (bi, i, 0)),
        out_shape=jax.ShapeDtypeStruct((b, s, D_B), F32),
        scratch_shapes=[pltpu.VMEM((ts + 2 * CONV_HALO, D_B), F32)],
        compiler_params=_params("parallel", "parallel"),
        name="glu_dwconv",
    )(zb, zb, zb, lw["conv_w"], lw["conv_b"])


def _dft_kernel(x_ref, cw_ref, dm_ref, o_ref, rhs_ref):
    i = pl.program_id(1)
    s = x_ref.shape[1]

    @pl.when(i == 0)
    def _():
        for g in range(D_C // GW_C):
            sl = slice(g * GW_C, (g + 1) * GW_C)
            cs = jnp.dot(x_ref[0, :, sl].astype(BF16), cw_ref[...], preferred_element_type=F32)
            rhs_ref[0:s, sl] = cs[:, :GW_C].astype(BF16)
            rhs_ref[s:2 * s, sl] = cs[:, GW_C:].astype(BF16)

    o_ref[0] = jnp.dot(dm_ref[...], rhs_ref[...], preferred_element_type=F32)


def _dft(zc, cw, dm, tm=256):
    b, s, _ = zc.shape
    return pl.pallas_call(
        _dft_kernel,
        grid=(b, s // tm),
        in_specs=[pl.BlockSpec((1, s, D_C), lambda bi, i: (bi, 0, 0)),
                  pl.BlockSpec(cw.shape, lambda bi, i: (0, 0)),
                  pl.BlockSpec((tm, 2 * s), lambda bi, i: (i, 0))],
        out_specs=pl.BlockSpec((1, tm, D_C), lambda bi, i: (bi, i, 0)),
        out_shape=jax.ShapeDtypeStruct((b, s, D_C), F32),
        scratch_shapes=[pltpu.VMEM((2 * s, D_C), BF16)],
        compiler_params=_params("parallel", "arbitrary"),
        name="dft2_real",
    )(zc, cw, dm)


def _dft_tables(s):
    def cs(n):
        idx = jnp.arange(n, dtype=jnp.int32)
        ang = ((idx[:, None] * idx[None, :]) % n).astype(F32) * (2.0 * math.pi / n)
        return jnp.cos(ang), jnp.sin(ang)
    cc, sc = cs(GW_C)
    cw = (jnp.concatenate([cc, sc], axis=1) * GW_C ** -0.5).astype(BF16)
    cp, sp = cs(s)
    dm = (jnp.concatenate([cp, -sp], axis=1) * s ** -0.5).astype(BF16)
    return cw, dm


def _merge_kernel(x_ref, y_ref, bon_ref, g_ref, c_ref, f_ref, zg_ref,
                  lnxg_ref, lnxb_ref, clng_ref, clnb_ref, gb_ref, bdm_ref,
                  wa_ref, wb_ref, wc_ref, wo_ref, o_ref):
    bdm = bdm_ref[...]
    ys = y_ref[0, 0] + y_ref[0, 1]
    dev = ys - _split_dot(ys, bdm)
    var = _split_dot(dev * dev, bdm)
    yn = dev * lax.rsqrt(var + EPS_GN) * lnxg_ref[...] + lnxb_ref[...]
    ya = _dot((yn + bon_ref[0]) * g_ref[0], wa_ref[...])

    c = c_ref[0]
    cd = c - jnp.mean(c, axis=-1, keepdims=True)
    cn = cd * lax.rsqrt(jnp.mean(cd * cd, axis=-1, keepdims=True) + EPS_LN) * clng_ref[...] + clnb_ref[...]
    yb = _dot(cn * jax.nn.sigmoid(cn), wb_ref[...])

    yc = _dot(f_ref[0], wc_ref[...])

    gates = jax.nn.sigmoid(zg_ref[0] + gb_ref[...])
    merged = (gates[:, 0:D_MODEL] * ya + gates[:, D_MODEL:2 * D_MODEL] * yb
              + gates[:, 2 * D_MODEL:3 * D_MODEL] * yc)
    o_ref[0] = x_ref[0] + _dot(merged, wo_ref[...])


def _merge(x, y, bon, g, c, f, zg, lw, tm=256):
    b, s, _ = x.shape
    full = lambda a: pl.BlockSpec(a.shape, lambda bi, i: (0,) * a.ndim)
    tok = lambda n: pl.BlockSpec((1, tm, n), lambda bi, i: (bi, i, 0))
    consts = (lw["lnx_g"], lw["lnx_b"], lw["conv_ln_g"], lw["conv_ln_b"], lw["gate_b"], lw["bd_mean"],
              lw["w_a_out"], lw["w_b_out"], lw["w_c_out"], lw["w_out"])
    return pl.pallas_call(
        _merge_kernel,
        grid=(b, s // tm),
        in_specs=[tok(D_MODEL), pl.BlockSpec((1, 2, tm, D_A), lambda bi, i: (bi, 0, i, 0)),
                  tok(D_A), tok(D_A), tok(D_B), tok(D_C), tok(3 * D_MODEL)] + [full(a) for a in consts],
        out_specs=tok(D_MODEL),
        out_shape=jax.ShapeDtypeStruct(x.shape, F32),
        compiler_params=_params("parallel", "parallel"),
        name="merge",
    )(x, y, bon, g, c, f, zg, *consts)


def _norm_proj_kernel(x_ref, g_ref, w_ref, o_ref):
    o_ref[...] = _dot(_rms(x_ref[...], g_ref[...]), w_ref[...]).astype(o_ref.dtype)


def _norm_proj(x2, g, w, out_dtype, tm=256):
    t = x2.shape[0]
    n = w.shape[1]
    return pl.pallas_call(
        _norm_proj_kernel,
        grid=(t // tm,),
        in_specs=[pl.BlockSpec((tm, D_MODEL), lambda i: (i, 0)),
                  pl.BlockSpec((1, D_MODEL), lambda i: (0, 0)),
                  pl.BlockSpec(w.shape, lambda i: (0, 0))],
        out_specs=pl.BlockSpec((tm, n), lambda i: (i, 0)),
        out_shape=jax.ShapeDtypeStruct((t, n), out_dtype),
        compiler_params=_params("parallel"),
        name="norm_proj",
    )(x2, g, w)


def _xattn_kernel(x_ref, kv_ref, g_ref, wq_ref, wo_ref, o_ref):
    x = x_ref[0]
    q = (_dot(_rms(x, g_ref[...]), wq_ref[...]) * HD_X ** -0.5).astype(BF16)
    heads = []
    for h in range(H_X):
        sl = slice(h * HD_X, (h + 1) * HD_X)
        sc = _dot_nt(q[:, sl], kv_ref[0, :, sl])
        e = jnp.exp(sc - jnp.max(sc, axis=-1, keepdims=True))
        p = e / jnp.sum(e, axis=-1, keepdims=True)
        heads.append(_dot(p, kv_ref[0, :, D_MODEL + h * HD_X:D_MODEL + (h + 1) * HD_X]))
    o_ref[0] = x + _dot(jnp.concatenate(heads, axis=1), wo_ref[...])


def _xattn(x, kv, lw, tm=256):
    b, s, _ = x.shape
    full = lambda a: pl.BlockSpec(a.shape, lambda bi, i: (0,) * a.ndim)
    return pl.pallas_call(
        _xattn_kernel,
        grid=(b, s // tm),
        in_specs=[pl.BlockSpec((1, tm, D_MODEL), lambda bi, i: (bi, i, 0)),
                  pl.BlockSpec((1, N_MEM, 2 * D_MODEL), lambda bi, i: (bi, 0, 0)),
                  full(lw["norm_x_g"]), full(lw["wq"]), full(lw["wo"])],
        out_specs=pl.BlockSpec((1, tm, D_MODEL), lambda bi, i: (bi, i, 0)),
        out_shape=jax.ShapeDtypeStruct(x.shape, F32),
        compiler_params=_params("parallel", "parallel"),
        name="cross_attn",
    )(x, kv, lw["norm_x_g"], lw["wq"], lw["wo"])


def _mlp_kernel(x_ref, g_ref, w1_ref, w2_ref, fg_ref, o_ref, hn_ref, acc_ref, *, final):
    j = pl.program_id(1)

    @pl.when(j == 0)
    def _():
        hn_ref[...] = _rms(x_ref[...], g_ref[...]).astype(BF16)
        acc_ref[...] = jnp.zeros_like(acc_ref)

    h1 = jnp.maximum(jnp.dot(hn_ref[...], w1_ref[...], preferred_element_type=F32), 0.0)
    acc_ref[...] += _dot(h1 * h1, w2_ref[...])

    @pl.when(j == pl.num_programs(1) - 1)
    def _():
        y = x_ref[...] + acc_ref[...]
        o_ref[...] = _rms(y, fg_ref[...]) if final else y


def _mlp(x2, lw, final_g, final, tm=512, tf=1024):
    t = x2.shape[0]
    return pl.pallas_call(
        functools.partial(_mlp_kernel, final=final),
        grid=(t // tm, D_FF // tf),
        in_specs=[pl.BlockSpec((tm, D_MODEL), lambda i, j: (i, 0)),
                  pl.BlockSpec((1, D_MODEL), lambda i, j: (0, 0)),
                  pl.BlockSpec((D_MODEL, tf), lambda i, j: (0, j)),
                  pl.BlockSpec((tf, D_MODEL), lambda i, j: (j, 0)),
                  pl.BlockSpec((1, D_MODEL), lambda i, j: (0, 0))],
        out_specs=pl.BlockSpec((tm, D_MODEL), lambda i, j: (i, 0)),
        out_shape=jax.ShapeDtypeStruct((t, D_MODEL), F32),
        scratch_shapes=[pltpu.VMEM((tm, D_MODEL), BF16), pltpu.VMEM((tm, D_MODEL), F32)],
        compiler_params=_params("parallel", "arbitrary"),
        name="mlp",
    )(x2, lw["norm_mlp_g"], lw["w_mlp1"], lw["w_mlp2"], final_g)


def _stage_layer(p, l):
    row = lambda a: a.reshape(1, -1).astype(F32)
    pad_lora = lambda w: jnp.stack([
        jnp.zeros((2 * LORA, D_A), F32).at[d * LORA:(d + 1) * LORA].set(w[d]) for d in range(2)]).astype(BF16)
    hid = jnp.arange(D_A, dtype=jnp.int32) // HEAD_A
    same_head = (hid[:, None] == hid[None, :]).astype(F32)
    return dict(
        norm_mix_g=row(p["norm_mix_g"][l]), w_in=p["w_in"][l].astype(BF16),
        mu=p["shift_mu"][l], w0=p["w0"][l], wup=pad_lora(p["w_up"][l]), a0=p["a0"][l],
        aup=pad_lora(p["a_up"][l]), gup=p["g_up"][l].astype(BF16),
        k_k=row(p["k_k"][l]), k_a=row(p["k_a"][l]), r_k=row(p["r_k"][l]),
        lnx_g=row(p["lnx_g"][l]), lnx_b=row(p["lnx_b"][l]),
        bd_ones=same_head.astype(BF16), bd_mean=(same_head / HEAD_A).astype(BF16),
        conv_w=jnp.concatenate([p["conv_w"][l], jnp.zeros((1, D_B), F32)], axis=0), conv_b=row(p["conv_b"][l]),
        conv_ln_g=row(p["conv_ln_g"][l]), conv_ln_b=row(p["conv_ln_b"][l]),
        gate_b=row(p["gate_b"][l]),
        w_a_out=p["w_a_out"][l].astype(BF16), w_b_out=p["w_b_out"][l].astype(BF16),
        w_c_out=p["w_c_out"][l].astype(BF16), w_out=p["w_out"][l].astype(BF16),
        norm_x_g=row(p["norm_x_g"][l]), norm_mem_g=row(p["norm_mem_g"][l]),
        wq=p["wq"][l].astype(BF16), wo=p["wo"][l].astype(BF16),
        wkv=jnp.concatenate([p["wk"][l], p["wv"][l]], axis=1).astype(BF16),
        norm_mlp_g=row(p["norm_mlp_g"][l]),
        w_mlp1=p["w_mlp1"][l].astype(BF16), w_mlp2=p["w_mlp2"][l].astype(BF16),
    )


def _trunk(x, mem, layers, final_g):
    b, s, _ = x.shape
    t = b * s
    cw, dm = _dft_tables(s)
    mem2 = mem.reshape(b * N_MEM, D_MODEL)
    for l, lw in enumerate(layers):
        za, zb, zc, zg = _in_proj(x.reshape(t, D_MODEL), lw["norm_mix_g"], lw["w_in"])
        r, v, na, g, bon, lwd, kd, bb = _rwkv_prep(za.reshape(b, s, N_A_IN), lw)
        y = _wkv(r, v, na, lwd, kd, bb)
        c = _conv(zb.reshape(b, s, 2 * D_B), lw)
        f = _dft(zc.reshape(b, s, D_C), cw, dm)
        x = _merge(x, y, bon, g, c, f, zg.reshape(b, s, 3 * D_MODEL), lw)
        kv = _norm_proj(mem2, lw["norm_mem_g"], lw["wkv"], BF16).reshape(b, N_MEM, 2 * D_MODEL)
        x = _xattn(x, kv, lw)
        x = _mlp(x.reshape(t, D_MODEL), lw, final_g, l == len(layers) - 1).reshape(b, s, D_MODEL)
    return x


def kernel(x_prompt, x_sample, mem_prompt, mem_sample, norm_mix_g, w_in, shift_mu, w0, w_up, a0, a_up, g_up, k_k, k_a, r_k, lnx_g, lnx_b, w_a_out, conv_w, conv_b, conv_ln_g, conv_ln_b, w_b_out, w_c_out, gate_b, w_out, norm_x_g, norm_mem_g, wq, wk, wv, wo, norm_mlp_g, w_mlp1, w_mlp2, final_norm_g):
    p = dict(norm_mix_g=norm_mix_g, w_in=w_in, shift_mu=shift_mu, w0=w0, w_up=w_up, a0=a0,
             a_up=a_up, g_up=g_up, k_k=k_k, k_a=k_a, r_k=r_k, lnx_g=lnx_g, lnx_b=lnx_b,
             w_a_out=w_a_out, conv_w=conv_w, conv_b=conv_b, conv_ln_g=conv_ln_g,
             conv_ln_b=conv_ln_b, w_b_out=w_b_out, w_c_out=w_c_out, gate_b=gate_b, w_out=w_out,
             norm_x_g=norm_x_g, norm_mem_g=norm_mem_g, wq=wq, wk=wk, wv=wv, wo=wo,
             norm_mlp_g=norm_mlp_g, w_mlp1=w_mlp1, w_mlp2=w_mlp2)
    layers = [_stage_layer(p, l) for l in range(DEPTH)]
    final_g = final_norm_g.reshape(1, D_MODEL).astype(F32)
    return (_trunk(x_prompt, mem_prompt, layers, final_g), _trunk(x_sample, mem_sample, layers, final_g))
```

```python
import functools
import math

import jax
import jax.numpy as jnp
from jax import lax
from jax.experimental import pallas as pl
from jax.experimental.pallas import tpu as pltpu

F32 = jnp.float32
BF16 = jnp.bfloat16

D_MODEL = 1024
DEPTH = 4
N_MEM = 256
D_A = 512
HEAD_A = 64
LORA = 64
LORA_G = 128
N_A_IN = 3 * D_A + 4 * LORA + LORA_G
D_B = 512
CONV_K = 31
D_C = 512
GW_C = 128
H_X = 4
HD_X = D_MODEL // H_X
D_FF = 4 * D_MODEL

EPS_RMS = 1e-6
EPS_LN = 1e-5
EPS_GN = 64e-5

VMEM_LIMIT_BYTES = 56 * 1024 * 1024

WKV_CHUNK = 64
WKV_HALF = 256
CONV_HALO = 16
SHIFT_HALO = 8
SUBLANES = 8


def _params(*sem):
    return pltpu.CompilerParams(dimension_semantics=sem, vmem_limit_bytes=VMEM_LIMIT_BYTES)


def _dot(a, b):
    return jnp.dot(a.astype(BF16), b.astype(BF16), preferred_element_type=F32)


def _dot_nt(a, b):
    return lax.dot_general(a.astype(BF16), b.astype(BF16), (((1,), (1,)), ((), ())),
                           preferred_element_type=F32)


def _split_dot(x, m):
    xh = x.astype(BF16)
    xl = (x - xh.astype(F32)).astype(BF16)
    return jnp.dot(xh, m, preferred_element_type=F32) + jnp.dot(xl, m, preferred_element_type=F32)


def _rms(x, g):
    return x * lax.rsqrt(jnp.mean(x * x, axis=-1, keepdims=True) + EPS_RMS) * g


def _in_proj_kernel(x_ref, g_ref, w_ref, gb_ref, za_ref, u_ref, zc_ref, gt_ref):
    hn = _rms(x_ref[...], g_ref[...]).astype(BF16)
    proj = lambda lo, n: jnp.dot(hn, w_ref[:, lo:lo + n], preferred_element_type=F32)
    za_ref[...] = proj(0, N_A_IN)
    zb = proj(N_A_IN, 2 * D_B)
    u_ref[...] = zb[:, :D_B] * jax.nn.sigmoid(zb[:, D_B:])
    zc_ref[...] = proj(N_A_IN + 2 * D_B, D_C).astype(BF16)
    gt_ref[...] = jax.nn.sigmoid(proj(N_A_IN + 2 * D_B + D_C, 3 * D_MODEL) + gb_ref[...]).astype(BF16)


def _in_proj(x2, g, w_in, gate_b, tm=256):
    t = x2.shape[0]
    outs = ((N_A_IN, F32), (D_B, F32), (D_C, BF16), (3 * D_MODEL, BF16))
    return pl.pallas_call(
        _in_proj_kernel,
        grid=(t // tm,),
        in_specs=[pl.BlockSpec((tm, D_MODEL), lambda i: (i, 0)),
                  pl.BlockSpec((1, D_MODEL), lambda i: (0, 0)),
                  pl.BlockSpec(w_in.shape, lambda i: (0, 0)),
                  pl.BlockSpec(gate_b.shape, lambda i: (0, 0))],
        out_specs=[pl.BlockSpec((tm, n), lambda i: (i, 0)) for n, _ in outs],
        out_shape=[jax.ShapeDtypeStruct((t, n), dt) for n, dt in outs],
        compiler_params=_params("parallel"),
        name="in_proj",
    )(x2, g, w_in, gate_b)


def _rwkv_prep_kernel(z_ref, zp_ref, zn_ref, mu_ref, w0_ref, wup_ref, a0_ref, aup_ref, gup_ref,
                      kk_ref, ka_ref, rk_ref, bd_ref,
                      r_ref, v_ref, na_ref, g_ref, bon_ref, lw_ref, kd_ref, b_ref):
    i = pl.program_id(1)
    n = pl.num_programs(1)
    z = z_ref[0]
    ts = z.shape[0]
    row = lax.broadcasted_iota(jnp.int32, z.shape, 0)
    prev_row = jnp.where(i > 0, zp_ref[0, SHIFT_HALO - 1:SHIFT_HALO, :], 0.0)
    next_row = jnp.where(i < n - 1, zn_ref[0, 0:1, :], 0.0)
    zprev = jnp.where(row == 0, prev_row, pltpu.roll(z, 1, axis=0))
    znext = jnp.where(row == ts - 1, next_row, pltpu.roll(z, ts - 1, axis=0))
    zs = z + mu_ref[0:1, :] * (zprev - z) + mu_ref[1:2, :] * (znext - z)

    r = zs[:, 0:D_A]
    k = zs[:, D_A:2 * D_A]
    v = zs[:, 2 * D_A:3 * D_A]
    o = 3 * D_A
    tdw = jnp.tanh(zs[:, o:o + 2 * LORA]).astype(BF16)
    dab = zs[:, o + 2 * LORA:o + 4 * LORA].astype(BF16)
    dg = zs[:, o + 4 * LORA:o + 4 * LORA + LORA_G]
    bd = bd_ref[...]

    g_ref[0] = _dot(jax.nn.sigmoid(dg), gup_ref[...])
    kk0 = k * kk_ref[...]
    nrm = jnp.sqrt(_split_dot(kk0 * kk0, bd))
    kk = kk0 / jnp.maximum(nrm, 1e-12)
    r_ref[0] = r
    v_ref[0] = v
    na_ref[0] = -kk
    ksum = jnp.zeros_like(k)
    for d in range(2):
        wl = w0_ref[d:d + 1, :] + jnp.dot(tdw, wup_ref[d], preferred_element_type=F32)
        lw_ref[0, d] = (-math.exp(-0.5)) * jax.nn.sigmoid(wl)
        a = jax.nn.sigmoid(a0_ref[d:d + 1, :] + jnp.dot(dab, aup_ref[d], preferred_element_type=F32))
        kd = k * (1.0 + (a - 1.0) * ka_ref[...])
        kd_ref[0, d] = kd
        b_ref[0, d] = kk * a
        ksum = ksum + kd
    bon_ref[0] = _split_dot(r * rk_ref[...] * ksum, bd) * v


def _rwkv_prep(za, lw, ts=256):
    b, s, _ = za.shape
    nh = ts // SHIFT_HALO
    last = s // SHIFT_HALO - 1
    full = lambda a: pl.BlockSpec(a.shape, lambda bi, i: (0,) * a.ndim)
    consts = (lw["mu"], lw["w0"], lw["wup"], lw["a0"], lw["aup"], lw["gup"], lw["k_k"], lw["k_a"],
              lw["r_k"], lw["bd_ones"])
    tok = pl.BlockSpec((1, ts, D_A), lambda bi, i: (bi, i, 0))
    dirs = pl.BlockSpec((1, 2, ts, D_A), lambda bi, i: (bi, 0, i, 0))
    tok_shape = jax.ShapeDtypeStruct((b, s, D_A), F32)
    dir_shape = jax.ShapeDtypeStruct((b, 2, s, D_A), F32)
    return pl.pallas_call(
        _rwkv_prep_kernel,
        grid=(b, s // ts),
        in_specs=[pl.BlockSpec((1, ts, N_A_IN), lambda bi, i: (bi, i, 0)),
                  pl.BlockSpec((1, SHIFT_HALO, N_A_IN), lambda bi, i: (bi, jnp.maximum(i * nh - 1, 0), 0)),
                  pl.BlockSpec((1, SHIFT_HALO, N_A_IN), lambda bi, i: (bi, jnp.minimum((i + 1) * nh, last), 0)),
                  ] + [full(c) for c in consts],
        out_specs=[tok] * 5 + [dirs] * 3,
        out_shape=[tok_shape] * 5 + [dir_shape] * 3,
        compiler_params=_params("parallel", "parallel"),
        name="rwkv_prep",
    )(za, za, za, *consts)


def _wkv_masks(reverse):
    C, HW = WKV_CHUNK, WKV_HALF
    row = lax.broadcasted_iota(jnp.int32, (C, HW), 0)
    lane_i = lax.broadcasted_iota(jnp.int32, (C, HW), 1) & (C - 1)
    dt = (lane_i - row) if reverse else (row - lane_i)
    ti = lax.broadcasted_iota(jnp.int32, (C, C), 0)
    tj = lax.broadcasted_iota(jnp.int32, (C, C), 1)
    rr = lax.broadcasted_iota(jnp.int32, (HW, HW), 0)
    cc = lax.broadcasted_iota(jnp.int32, (HW, HW), 1)
    return dict(tri=jnp.where((tj >= ti) if reverse else (tj <= ti), 1.0, 0.0).astype(BF16),
                strict=dt > 0, incl=dt >= 0, eye=dt == 0,
                bd=(rr >> 6) == (cc >> 6), diag=rr == cc)


def _wkv_chunks(chains):
    C, HW = WKV_CHUNK, WKV_HALF
    n = range(len(chains))
    r, a, b, k, v, lw, sbd, m, rev = zip(*chains)
    bd = lambda i, x: jnp.where(m[i]["bd"], jnp.concatenate([x] * (HW // C), axis=0), 0.0)

    def cumsum(i):
        l1 = lw[i].astype(BF16)
        r1 = lw[i] - l1.astype(F32)
        l2 = r1.astype(BF16)
        l3 = (r1 - l2.astype(F32)).astype(BF16)
        tri = m[i]["tri"]
        return (jnp.dot(tri, l1, preferred_element_type=F32) + jnp.dot(tri, l2, preferred_element_type=F32)
                + jnp.dot(tri, l3, preferred_element_type=F32))

    cum = [cumsum(i) for i in n]
    tot = [cum[i][0:1, :] if rev[i] else cum[i][C - 1:C, :] for i in n]
    e_inv = [jnp.exp(-cum[i]) for i in n]
    e_rem = [jnp.exp(tot[i] - cum[i]) for i in n]
    at = [a[i] * jnp.exp(cum[i] - lw[i]) for i in n]
    rt = [r[i] * jnp.exp(cum[i]) for i in n]

    gram = [_dot_nt(jnp.concatenate([at[i], rt[i]], axis=0),
                    jnp.concatenate([bd(i, b[i] * e_inv[i]), bd(i, k[i] * e_inv[i])], axis=0)) for i in n]
    l_ab = [jnp.where(m[i]["strict"], gram[i][:C, :HW], 0.0) for i in n]
    l_ak = [jnp.where(m[i]["strict"], gram[i][:C, HW:], 0.0) for i in n]
    m_rb = [jnp.where(m[i]["incl"], gram[i][C:, :HW], 0.0) for i in n]
    m_rk = [jnp.where(m[i]["incl"], gram[i][C:, HW:], 0.0) for i in n]

    t_inv = [jnp.where(m[i]["eye"], 1.0, 0.0) + l_ab[i] for i in n]
    lp = [_dot(l_ab[i], bd(i, l_ab[i])) for i in n]
    nsq = int(math.log2(C)) - 1
    for j in range(1, nsq):
        res = [_dot(jnp.concatenate([lp[i], t_inv[i]], axis=0), bd(i, lp[i])) for i in n]
        lp = [res[i][:C] for i in n]
        t_inv = [t_inv[i] + res[i][C:] for i in n]
    t_inv = [t_inv[i] + _dot(t_inv[i], bd(i, lp[i])) for i in n]

    lv = [_dot(jnp.concatenate([l_ak[i], m_rk[i]], axis=0), bd(i, v[i])) for i in n]
    pq = [_dot(t_inv[i], jnp.concatenate([bd(i, at[i]), bd(i, lv[i][:C])], axis=1)) for i in n]
    mpq = [_dot(m_rb[i], jnp.concatenate([bd(i, pq[i][:, :HW]), bd(i, pq[i][:, HW:])], axis=1)) for i in n]
    rp = [rt[i] + mpq[i][:, :HW] for i in n]
    yq = [mpq[i][:, HW:] + lv[i][C:] for i in n]

    def state_maps(i):
        lhs_t = jnp.transpose(jnp.concatenate([b[i] * e_rem[i], k[i] * e_rem[i]], axis=0))
        rhs = jnp.concatenate([pq[i], jnp.concatenate([jnp.zeros_like(v[i]), v[i]], axis=1)], axis=0)
        return _dot(lhs_t, rhs)

    gh = [state_maps(i) for i in n]
    wc_col = [jnp.sum(jnp.where(m[i]["diag"], jnp.exp(tot[i]), 0.0), axis=1, keepdims=True) for i in n]
    gbd = [jnp.where(m[i]["bd"], gh[i][:, :HW], 0.0) + jnp.where(m[i]["diag"], wc_col[i], 0.0) for i in n]
    out = [_dot(jnp.concatenate([gbd[i], rp[i]], axis=0), sbd[i]) for i in n]
    return [(out[i][HW:] + yq[i], out[i][:HW] + jnp.where(m[i]["bd"], gh[i][:, HW:], 0.0)) for i in n]


def _wkv_kernel(rf_ref, vf_ref, af_ref, lwf_ref, kdf_ref, bf_ref,
                rb_ref, vb_ref, ab_ref, lwb_ref, kdb_ref, bb_ref, yf_ref, yb_ref, s_ref, *, nb):
    @pl.when(pl.program_id(1) == 0)
    def _():
        s_ref[...] = jnp.zeros_like(s_ref)

    dirs = ((rf_ref, vf_ref, af_ref, lwf_ref, kdf_ref, bf_ref, yf_ref),
            (rb_ref, vb_ref, ab_ref, lwb_ref, kdb_ref, bb_ref, yb_ref))
    masks = (_wkv_masks(False), _wkv_masks(True))
    chains, dests = [], []
    for bi in range(nb):
        for d, (r_ref, v_ref, a_ref, lw_ref, kd_ref, b_ref, y_ref) in enumerate(dirs):
            for h in range(D_A // WKV_HALF):
                sl = slice(h * WKV_HALF, (h + 1) * WKV_HALF)
                chains.append((r_ref[bi, :, sl], a_ref[bi, :, sl], b_ref[bi, 0, :, sl], kd_ref[bi, 0, :, sl],
                               v_ref[bi, :, sl], lw_ref[bi, 0, :, sl], s_ref[bi, d, h], masks[d], d == 1))
                dests.append((y_ref, bi, sl, d, h))
    for (y, s_new), (y_ref, bi, sl, d, h) in zip(_wkv_chunks(chains), dests):
        y_ref[bi, :, sl] = y
        s_ref[bi, d, h] = s_new


def _wkv(r, v, na, lw, kd, bb, nb=2):
    b, s, _ = r.shape
    assert b % nb == 0
    nc = s // WKV_CHUNK
    tok_f = pl.BlockSpec((nb, WKV_CHUNK, D_A), lambda bi, c: (bi, c, 0))
    tok_b = pl.BlockSpec((nb, WKV_CHUNK, D_A), lambda bi, c: (bi, nc - 1 - c, 0))
    dir_f = pl.BlockSpec((nb, 1, WKV_CHUNK, D_A), lambda bi, c: (bi, 0, c, 0))
    dir_b = pl.BlockSpec((nb, 1, WKV_CHUNK, D_A), lambda bi, c: (bi, 1, nc - 1 - c, 0))
    y_shape = jax.ShapeDtypeStruct((b, s, D_A), F32)
    return pl.pallas_call(
        functools.partial(_wkv_kernel, nb=nb),
        grid=(b // nb, nc),
        in_specs=[tok_f] * 3 + [dir_f] * 3 + [tok_b] * 3 + [dir_b] * 3,
        out_specs=[tok_f, tok_b],
        out_shape=[y_shape, y_shape],
        scratch_shapes=[pltpu.VMEM((nb, 2, D_A // WKV_HALF, WKV_HALF, WKV_HALF), F32)],
        compiler_params=_params("parallel", "arbitrary"),
        name="wkv_scan",
    )(r, v, na, lw, kd, bb, r, v, na, lw, kd, bb)


def _conv_kernel(x_ref, xp_ref, xn_ref, w_ref, b_ref, c_ref, u_ref, *, sub):
    i = pl.program_id(1)
    n = pl.num_programs(1)
    ts = x_ref.shape[1]
    rows = ts + 2 * CONV_HALO
    u_ref[0, 0:CONV_HALO, :] = jnp.where(i > 0, xp_ref[0], 0.0)
    u_ref[0, CONV_HALO:CONV_HALO + ts, :] = x_ref[0]
    u_ref[0, CONV_HALO + ts:rows, :] = jnp.where(i < n - 1, xn_ref[0], 0.0)
    for s in range(1, SUBLANES):
        u_ref[s, 0:rows - SUBLANES, :] = u_ref[0, s:s + rows - SUBLANES, :]
    first = CONV_HALO - CONV_K // 2
    for t0 in range(0, ts, sub):
        acc = jnp.zeros((sub, D_B), F32) + b_ref[...]
        for j in range(CONV_K):
            q, s = divmod(first + j, SUBLANES)
            acc = acc + w_ref[j:j + 1, :] * u_ref[s, t0 + q * SUBLANES:t0 + q * SUBLANES + sub, :]
        c_ref[0, t0:t0 + sub, :] = acc


def _conv(u, lw, ts=256, sub=32):
    b, s, _ = u.shape
    nh = ts // CONV_HALO
    last = s // CONV_HALO - 1
    return pl.pallas_call(
        functools.partial(_conv_kernel, sub=sub),
        grid=(b, s // ts),
        in_specs=[pl.BlockSpec((1, ts, D_B), lambda bi, i: (bi, i, 0)),
                  pl.BlockSpec((1, CONV_HALO, D_B), lambda bi, i: (bi, jnp.maximum(i * nh - 1, 0), 0)),
                  pl.BlockSpec((1, CONV_HALO, D_B), lambda bi, i: (bi, jnp.minimum((i + 1) * nh, last), 0)),
                  pl.BlockSpec((CONV_K + 1, D_B), lambda bi, i: (0, 0)),
                  pl.BlockSpec((1, D_B), lambda bi, i: (0, 0))],
        out_specs=pl.BlockSpec((1, ts, D_B), lambda bi, i: (bi, i, 0)),
        out_shape=jax.ShapeDtypeStruct((b, s, D_B), F32),
        scratch_shapes=[pltpu.VMEM((SUBLANES, ts + 2 * CONV_HALO, D_B), F32)],
        compiler_params=_params("parallel", "parallel"),
        name="dwconv",
    )(u, u, u, lw["conv_w"], lw["conv_b"])


def _dft_kernel(x_ref, cw_ref, dm_ref, o_ref, rhs_ref):
    i = pl.program_id(1)
    s = x_ref.shape[1]

    @pl.when(i == 0)
    def _():
        for g in range(D_C // GW_C):
            sl = slice(g * GW_C, (g + 1) * GW_C)
            cs = jnp.dot(x_ref[0, :, sl], cw_ref[...], preferred_element_type=F32)
            rhs_ref[0:s, sl] = cs[:, :GW_C].astype(BF16)
            rhs_ref[s:2 * s, sl] = cs[:, GW_C:].astype(BF16)

    o_ref[0] = jnp.dot(dm_ref[...], rhs_ref[...], preferred_element_type=F32)


def _dft(zc, cw, dm, tm=512):
    b, s, _ = zc.shape
    return pl.pallas_call(
        _dft_kernel,
        grid=(b, s // tm),
        in_specs=[pl.BlockSpec((1, s, D_C), lambda bi, i: (bi, 0, 0)),
                  pl.BlockSpec(cw.shape, lambda bi, i: (0, 0)),
                  pl.BlockSpec((tm, 2 * s), lambda bi, i: (i, 0))],
        out_specs=pl.BlockSpec((1, tm, D_C), lambda bi, i: (bi, i, 0)),
        out_shape=jax.ShapeDtypeStruct((b, s, D_C), F32),
        scratch_shapes=[pltpu.VMEM((2 * s, D_C), BF16)],
        compiler_params=_params("parallel", "arbitrary"),
        name="dft2_real",
    )(zc, cw, dm)


def _dft_tables(s):
    def cs(n):
        idx = jnp.arange(n, dtype=jnp.int32)
        ang = ((idx[:, None] * idx[None, :]) % n).astype(F32) * (2.0 * math.pi / n)
        return jnp.cos(ang), jnp.sin(ang)
    cc, sc = cs(GW_C)
    cw = (jnp.concatenate([cc, sc], axis=1) * GW_C ** -0.5).astype(BF16)
    cp, sp = cs(s)
    dm = (jnp.concatenate([cp, -sp], axis=1) * s ** -0.5).astype(BF16)
    return cw, dm


def _merge_kernel(x_ref, yf_ref, yb_ref, bon_ref, g_ref, c_ref, f_ref, gt_ref,
                  lnxg_ref, lnxb_ref, clng_ref, clnb_ref, bdm_ref,
                  wa_ref, wb_ref, wc_ref, wo_ref, o_ref):
    bdm = bdm_ref[...]
    ys = yf_ref[0] + yb_ref[0]
    dev = ys - _split_dot(ys, bdm)
    var = _split_dot(dev * dev, bdm)
    yn = dev * lax.rsqrt(var + EPS_GN) * lnxg_ref[...] + lnxb_ref[...]
    ya = _dot((yn + bon_ref[0]) * g_ref[0], wa_ref[...])

    c = c_ref[0]
    cd = c - jnp.mean(c, axis=-1, keepdims=True)
    cn = cd * lax.rsqrt(jnp.mean(cd * cd, axis=-1, keepdims=True) + EPS_LN) * clng_ref[...] + clnb_ref[...]
    yb = _dot(cn * jax.nn.sigmoid(cn), wb_ref[...])

    yc = _dot(f_ref[0], wc_ref[...])

    gate = lambda j: gt_ref[0, :, j * D_MODEL:(j + 1) * D_MODEL].astype(F32)
    merged = gate(0) * ya + gate(1) * yb + gate(2) * yc
    o_ref[0] = x_ref[0] + _dot(merged, wo_ref[...])


def _merge(x, yf, yb, bon, g, c, f, gates, lw, tm=256):
    b, s, _ = x.shape
    full = lambda a: pl.BlockSpec(a.shape, lambda bi, i: (0,) * a.ndim)
    tok = lambda n: pl.BlockSpec((1, tm, n), lambda bi, i: (bi, i, 0))
    consts = (lw["lnx_g"], lw["lnx_b"], lw["conv_ln_g"], lw["conv_ln_b"], lw["bd_mean"],
              lw["w_a_out"], lw["w_b_out"], lw["w_c_out"], lw["w_out"])
    return pl.pallas_call(
        _merge_kernel,
        grid=(b, s // tm),
        in_specs=[tok(D_MODEL), tok(D_A), tok(D_A), tok(D_A), tok(D_A), tok(D_B), tok(D_C), tok(3 * D_MODEL)] + [full(a) for a in consts],
        out_specs=tok(D_MODEL),
        out_shape=jax.ShapeDtypeStruct(x.shape, F32),
        compiler_params=_params("parallel", "parallel"),
        name="merge",
    )(x, yf, yb, bon, g, c, f, gates, *consts)


def _norm_proj_kernel(x_ref, g_ref, w_ref, o_ref):
    o_ref[...] = _dot(_rms(x_ref[...], g_ref[...]), w_ref[...]).astype(o_ref.dtype)


def _norm_proj(x2, g, w, out_dtype, tm=256):
    t = x2.shape[0]
    n = w.shape[1]
    return pl.pallas_call(
        _norm_proj_kernel,
        grid=(t // tm,),
        in_specs=[pl.BlockSpec((tm, D_MODEL), lambda i: (i, 0)),
                  pl.BlockSpec((1, D_MODEL), lambda i: (0, 0)),
                  pl.BlockSpec(w.shape, lambda i: (0, 0))],
        out_specs=pl.BlockSpec((tm, n), lambda i: (i, 0)),
        out_shape=jax.ShapeDtypeStruct((t, n), out_dtype),
        compiler_params=_params("parallel"),
        name="norm_proj",
    )(x2, g, w)


def _xattn_kernel(x_ref, kv_ref, g_ref, wq_ref, wo_ref, o_ref):
    x = x_ref[0]
    q = (_dot(_rms(x, g_ref[...]), wq_ref[...]) * HD_X ** -0.5).astype(BF16)
    heads = []
    for h in range(H_X):
        sl = slice(h * HD_X, (h + 1) * HD_X)
        sc = _dot_nt(q[:, sl], kv_ref[0, :, sl])
        e = jnp.exp(sc - jnp.max(sc, axis=-1, keepdims=True))
        p = e / jnp.sum(e, axis=-1, keepdims=True)
        heads.append(_dot(p, kv_ref[0, :, D_MODEL + h * HD_X:D_MODEL + (h + 1) * HD_X]))
    o_ref[0] = x + _dot(jnp.concatenate(heads, axis=1), wo_ref[...])


def _xattn(x, kv, lw, tm=512):
    b, s, _ = x.shape
    full = lambda a: pl.BlockSpec(a.shape, lambda bi, i: (0,) * a.ndim)
    return pl.pallas_call(
        _xattn_kernel,
        grid=(b, s // tm),
        in_specs=[pl.BlockSpec((1, tm, D_MODEL), lambda bi, i: (bi, i, 0)),
                  pl.BlockSpec((1, N_MEM, 2 * D_MODEL), lambda bi, i: (bi, 0, 0)),
                  full(lw["norm_x_g"]), full(lw["wq"]), full(lw["wo"])],
        out_specs=pl.BlockSpec((1, tm, D_MODEL), lambda bi, i: (bi, i, 0)),
        out_shape=jax.ShapeDtypeStruct(x.shape, F32),
        compiler_params=_params("parallel", "parallel"),
        name="cross_attn",
    )(x, kv, lw["norm_x_g"], lw["wq"], lw["wo"])


def _mlp_kernel(x_ref, g_ref, w1_ref, w2_ref, fg_ref, o_ref, hn_ref, acc_ref, *, final):
    j = pl.program_id(1)

    @pl.when(j == 0)
    def _():
        hn_ref[...] = _rms(x_ref[...], g_ref[...]).astype(BF16)
        acc_ref[...] = jnp.zeros_like(acc_ref)

    h1 = jnp.maximum(jnp.dot(hn_ref[...], w1_ref[...], preferred_element_type=F32), 0.0)
    acc_ref[...] += _dot(h1 * h1, w2_ref[...])

    @pl.when(j == pl.num_programs(1) - 1)
    def _():
        y = x_ref[...] + acc_ref[...]
        o_ref[...] = _rms(y, fg_ref[...]) if final else y


def _mlp(x2, lw, final_g, final, tm=1024, tf=1024):
    t = x2.shape[0]
    return pl.pallas_call(
        functools.partial(_mlp_kernel, final=final),
        grid=(t // tm, D_FF // tf),
        in_specs=[pl.BlockSpec((tm, D_MODEL), lambda i, j: (i, 0)),
                  pl.BlockSpec((1, D_MODEL), lambda i, j: (0, 0)),
                  pl.BlockSpec((D_MODEL, tf), lambda i, j: (0, j)),
                  pl.BlockSpec((tf, D_MODEL), lambda i, j: (j, 0)),
                  pl.BlockSpec((1, D_MODEL), lambda i, j: (0, 0))],
        out_specs=pl.BlockSpec((tm, D_MODEL), lambda i, j: (i, 0)),
        out_shape=jax.ShapeDtypeStruct((t, D_MODEL), F32),
        scratch_shapes=[pltpu.VMEM((tm, D_MODEL), BF16), pltpu.VMEM((tm, D_MODEL), F32)],
        compiler_params=_params("parallel", "arbitrary"),
        name="mlp",
    )(x2, lw["norm_mlp_g"], lw["w_mlp1"], lw["w_mlp2"], final_g)


def _stage_layer(p, l):
    row = lambda a: a.reshape(1, -1).astype(F32)
    pad_lora = lambda w: jnp.stack([
        jnp.zeros((2 * LORA, D_A), F32).at[d * LORA:(d + 1) * LORA].set(w[d]) for d in range(2)]).astype(BF16)
    hid = jnp.arange(D_A, dtype=jnp.int32) // HEAD_A
    same_head = (hid[:, None] == hid[None, :]).astype(F32)
    return dict(
        norm_mix_g=row(p["norm_mix_g"][l]), w_in=p["w_in"][l].astype(BF16),
        mu=p["shift_mu"][l], w0=p["w0"][l], wup=pad_lora(p["w_up"][l]), a0=p["a0"][l],
        aup=pad_lora(p["a_up"][l]), gup=p["g_up"][l].astype(BF16),
        k_k=row(p["k_k"][l]), k_a=row(p["k_a"][l]), r_k=row(p["r_k"][l]),
        lnx_g=row(p["lnx_g"][l]), lnx_b=row(p["lnx_b"][l]),
        bd_ones=same_head.astype(BF16), bd_mean=(same_head / HEAD_A).astype(BF16),
        conv_w=jnp.concatenate([p["conv_w"][l], jnp.zeros((1, D_B), F32)], axis=0), conv_b=row(p["conv_b"][l]),
        conv_ln_g=row(p["conv_ln_g"][l]), conv_ln_b=row(p["conv_ln_b"][l]),
        gate_b=row(p["gate_b"][l]),
        w_a_out=p["w_a_out"][l].astype(BF16), w_b_out=p["w_b_out"][l].astype(BF16),
        w_c_out=p["w_c_out"][l].astype(BF16), w_out=p["w_out"][l].astype(BF16),
        norm_x_g=row(p["norm_x_g"][l]), norm_mem_g=row(p["norm_mem_g"][l]),
        wq=p["wq"][l].astype(BF16), wo=p["wo"][l].astype(BF16),
        wkv=jnp.concatenate([p["wk"][l], p["wv"][l]], axis=1).astype(BF16),
        norm_mlp_g=row(p["norm_mlp_g"][l]),
        w_mlp1=p["w_mlp1"][l].astype(BF16), w_mlp2=p["w_mlp2"][l].astype(BF16),
    )


def _trunk(x, mem, layers, final_g):
    b, s, _ = x.shape
    t = b * s
    cw, dm = _dft_tables(s)
    mem2 = mem.reshape(b * N_MEM, D_MODEL)
    for l, lw in enumerate(layers):
        za, u, zc, gates = _in_proj(x.reshape(t, D_MODEL), lw["norm_mix_g"], lw["w_in"], lw["gate_b"])
        r, v, na, g, bon, lwd, kd, bb = _rwkv_prep(za.reshape(b, s, N_A_IN), lw)
        yf, yb = _wkv(r, v, na, lwd, kd, bb)
        c = _conv(u.reshape(b, s, D_B), lw)
        f = _dft(zc.reshape(b, s, D_C), cw, dm)
        x = _merge(x, yf, yb, bon, g, c, f, gates.reshape(b, s, 3 * D_MODEL), lw)
        kv = _norm_proj(mem2, lw["norm_mem_g"], lw["wkv"], BF16).reshape(b, N_MEM, 2 * D_MODEL)
        x = _xattn(x, kv, lw)
        x = _mlp(x.reshape(t, D_MODEL), lw, final_g, l == len(layers) - 1).reshape(b, s, D_MODEL)
    return x


def kernel(x_prompt, x_sample, mem_prompt, mem_sample, norm_mix_g, w_in, shift_mu, w0, w_up, a0, a_up, g_up, k_k, k_a, r_k, lnx_g, lnx_b, w_a_out, conv_w, conv_b, conv_ln_g, conv_ln_b, w_b_out, w_c_out, gate_b, w_out, norm_x_g, norm_mem_g, wq, wk, wv, wo, norm_mlp_g, w_mlp1, w_mlp2, final_norm_g):
    p = dict(norm_mix_g=norm_mix_g, w_in=w_in, shift_mu=shift_mu, w0=w0, w_up=w_up, a0=a0,
             a_up=a_up, g_up=g_up, k_k=k_k, k_a=k_a, r_k=r_k, lnx_g=lnx_g, lnx_b=lnx_b,
             w_a_out=w_a_out, conv_w=conv_w, conv_b=conv_b, conv_ln_g=conv_ln_g,
             conv_ln_b=conv_ln_b, w_b_out=w_b_out, w_c_out=w_c_out, gate_b=gate_b, w_out=w_out,
             norm_x_g=norm_x_g, norm_mem_g=norm_mem_g, wq=wq, wk=wk, wv=wv, wo=wo,
             norm_mlp_g=norm_mlp_g, w_mlp1=w_mlp1, w_mlp2=w_mlp2)
    layers = [_stage_layer(p, l) for l in range(DEPTH)]
    final_g = final_norm_g.reshape(1, D_MODEL).astype(F32)
    return (_trunk(x_prompt, mem_prompt, layers, final_g), _trunk(x_sample, mem_sample, layers, final_g))
```

```python
import functools
import math

import jax
import jax.numpy as jnp
from jax import lax
from jax.experimental import pallas as pl
from jax.experimental.pallas import tpu as pltpu

F32 = jnp.float32
BF16 = jnp.bfloat16

D_MODEL = 1024
DEPTH = 4
N_MEM = 256
D_A = 512
HEAD_A = 64
LORA = 64
LORA_G = 128
N_A_IN = 3 * D_A + 4 * LORA + LORA_G
D_B = 512
CONV_K = 31
D_C = 512
GW_C = 128
H_X = 4
HD_X = D_MODEL // H_X
D_FF = 4 * D_MODEL

EPS_RMS = 1e-6
EPS_LN = 1e-5
EPS_GN = 64e-5

VMEM_LIMIT_BYTES = 56 * 1024 * 1024

WKV_CHUNK = 64
WKV_HALF = 128
CONV_HALO = 16
SHIFT_HALO = 8
SUBLANES = 8
DFT_BLK = 256
NYQ_ROWS = 16


def _params(*sem):
    return pltpu.CompilerParams(dimension_semantics=sem, vmem_limit_bytes=VMEM_LIMIT_BYTES)


def _tiles(n, tile):
    assert n % tile == 0, (n, tile)
    return n // tile


def _dot(a, b):
    return jnp.dot(a.astype(BF16), b.astype(BF16), preferred_element_type=F32)


def _dot_nt(a, b):
    return lax.dot_general(a.astype(BF16), b.astype(BF16), (((1,), (1,)), ((), ())),
                           preferred_element_type=F32)


def _split_dot(x, m):
    xh = x.astype(BF16)
    xl = (x - xh.astype(F32)).astype(BF16)
    return jnp.dot(xh, m, preferred_element_type=F32) + jnp.dot(xl, m, preferred_element_type=F32)


def _split_dot_lhs(m, x):
    xh = x.astype(BF16)
    xl = (x - xh.astype(F32)).astype(BF16)
    return jnp.dot(m, xh, preferred_element_type=F32) + jnp.dot(m, xl, preferred_element_type=F32)


def _rms(x, g):
    return x * lax.rsqrt(jnp.mean(x * x, axis=-1, keepdims=True) + EPS_RMS) * g


def _in_proj_kernel(x_ref, g_ref, w_ref, gb_ref, za_ref, u_ref, zc_ref, gt_ref):
    hn = _rms(x_ref[...], g_ref[...]).astype(BF16)
    proj = lambda lo, n: jnp.dot(hn, w_ref[:, lo:lo + n], preferred_element_type=F32)
    za_ref[...] = proj(0, N_A_IN)
    zb = proj(N_A_IN, 2 * D_B)
    u_ref[...] = zb[:, :D_B] * jax.nn.sigmoid(zb[:, D_B:])
    zc_ref[...] = proj(N_A_IN + 2 * D_B, D_C).astype(BF16)
    gt_ref[...] = jax.nn.sigmoid(proj(N_A_IN + 2 * D_B + D_C, 3 * D_MODEL) + gb_ref[...]).astype(BF16)


def _in_proj(x2, g, w_in, gate_b, tm=256):
    t = x2.shape[0]
    outs = ((N_A_IN, F32), (D_B, F32), (D_C, BF16), (3 * D_MODEL, BF16))
    return pl.pallas_call(
        _in_proj_kernel,
        grid=(_tiles(t, tm),),
        in_specs=[pl.BlockSpec((tm, D_MODEL), lambda i: (i, 0)),
                  pl.BlockSpec((1, D_MODEL), lambda i: (0, 0)),
                  pl.BlockSpec(w_in.shape, lambda i: (0, 0)),
                  pl.BlockSpec(gate_b.shape, lambda i: (0, 0))],
        out_specs=[pl.BlockSpec((tm, n), lambda i: (i, 0)) for n, _ in outs],
        out_shape=[jax.ShapeDtypeStruct((t, n), dt) for n, dt in outs],
        compiler_params=_params("parallel"),
        name="in_proj",
    )(x2, g, w_in, gate_b)


def _rwkv_prep_kernel(z_ref, zp_ref, zn_ref, mu_ref, w0_ref, wup_ref, a0_ref, aup_ref, gup_ref,
                      kk_ref, ka_ref, rk_ref, bd_ref,
                      r_ref, v_ref, na_ref, g_ref, bon_ref, lw_ref, kd_ref, b_ref):
    i = pl.program_id(1)
    n = pl.num_programs(1)
    z = z_ref[0]
    ts = z.shape[0]
    row = lax.broadcasted_iota(jnp.int32, z.shape, 0)
    prev_row = jnp.where(i > 0, zp_ref[0, SHIFT_HALO - 1:SHIFT_HALO, :], 0.0)
    next_row = jnp.where(i < n - 1, zn_ref[0, 0:1, :], 0.0)
    zprev = jnp.where(row == 0, prev_row, pltpu.roll(z, 1, axis=0))
    znext = jnp.where(row == ts - 1, next_row, pltpu.roll(z, ts - 1, axis=0))
    zs = z + mu_ref[0:1, :] * (zprev - z) + mu_ref[1:2, :] * (znext - z)

    r = zs[:, 0:D_A]
    k = zs[:, D_A:2 * D_A]
    v = zs[:, 2 * D_A:3 * D_A]
    o = 3 * D_A
    tdw = jnp.tanh(zs[:, o:o + 2 * LORA]).astype(BF16)
    dab = zs[:, o + 2 * LORA:o + 4 * LORA].astype(BF16)
    dg = zs[:, o + 4 * LORA:o + 4 * LORA + LORA_G]
    bd = bd_ref[...]

    g_ref[0] = _dot(jax.nn.sigmoid(dg), gup_ref[...])
    kk0 = k * kk_ref[...]
    nrm = jnp.sqrt(_dot(kk0 * kk0, bd))
    kk = kk0 / jnp.maximum(nrm, 1e-12)
    r_ref[0] = r
    v_ref[0] = v
    na_ref[0] = -kk
    ksum = jnp.zeros_like(k)
    for d in range(2):
        wl = w0_ref[d:d + 1, :] + jnp.dot(tdw, wup_ref[d], preferred_element_type=F32)
        lw_ref[0, d] = (-math.exp(-0.5)) * jax.nn.sigmoid(wl)
        a = jax.nn.sigmoid(a0_ref[d:d + 1, :] + jnp.dot(dab, aup_ref[d], preferred_element_type=F32))
        kd = k * (1.0 + (a - 1.0) * ka_ref[...])
        kd_ref[0, d] = kd
        b_ref[0, d] = kk * a
        ksum = ksum + kd
    bon_ref[0] = _dot(r * rk_ref[...] * ksum, bd) * v


def _rwkv_prep(za, lw, ts=256):
    b, s, _ = za.shape
    nh = ts // SHIFT_HALO
    last = s // SHIFT_HALO - 1
    full = lambda a: pl.BlockSpec(a.shape, lambda bi, i: (0,) * a.ndim)
    consts = (lw["mu"], lw["w0"], lw["wup"], lw["a0"], lw["aup"], lw["gup"], lw["k_k"], lw["k_a"],
              lw["r_k"], lw["bd_ones"])
    tok = pl.BlockSpec((1, ts, D_A), lambda bi, i: (bi, i, 0))
    dirs = pl.BlockSpec((1, 2, ts, D_A), lambda bi, i: (bi, 0, i, 0))
    tok_shape = jax.ShapeDtypeStruct((b, s, D_A), F32)
    dir_shape = jax.ShapeDtypeStruct((b, 2, s, D_A), F32)
    return pl.pallas_call(
        _rwkv_prep_kernel,
        grid=(b, _tiles(s, ts)),
        in_specs=[pl.BlockSpec((1, ts, N_A_IN), lambda bi, i: (bi, i, 0)),
                  pl.BlockSpec((1, SHIFT_HALO, N_A_IN), lambda bi, i: (bi, jnp.maximum(i * nh - 1, 0), 0)),
                  pl.BlockSpec((1, SHIFT_HALO, N_A_IN), lambda bi, i: (bi, jnp.minimum((i + 1) * nh, last), 0)),
                  ] + [full(c) for c in consts],
        out_specs=[tok] * 5 + [dirs] * 3,
        out_shape=[tok_shape] * 5 + [dir_shape] * 3,
        compiler_params=_params("parallel", "parallel"),
        name="rwkv_prep",
    )(za, za, za, *consts)


def _wkv_masks(reverse):
    C, HW = WKV_CHUNK, WKV_HALF
    row = lax.broadcasted_iota(jnp.int32, (C, HW), 0)
    lane_i = lax.broadcasted_iota(jnp.int32, (C, HW), 1) & (C - 1)
    dt = (lane_i - row) if reverse else (row - lane_i)
    ti = lax.broadcasted_iota(jnp.int32, (C, C), 0)
    tj = lax.broadcasted_iota(jnp.int32, (C, C), 1)
    rr = lax.broadcasted_iota(jnp.int32, (HW, HW), 0)
    cc = lax.broadcasted_iota(jnp.int32, (HW, HW), 1)
    return dict(tri=jnp.where((tj >= ti) if reverse else (tj <= ti), 1.0, 0.0).astype(BF16),
                strict=dt > 0, incl=dt >= 0, eye=dt == 0,
                bd=(rr >> 6) == (cc >> 6), diag=rr == cc)


def _wkv_chunks(chains):
    C, HW = WKV_CHUNK, WKV_HALF
    n = range(len(chains))
    r, a, b, k, v, lw, sbd, m, rev = zip(*chains)
    bd = lambda i, x: jnp.where(m[i]["bd"], jnp.concatenate([x] * (HW // C), axis=0), 0.0)

    cum = [_split_dot_lhs(m[i]["tri"], lw[i]) for i in n]
    tot = [cum[i][0:1, :] if rev[i] else cum[i][C - 1:C, :] for i in n]
    e_inv = [jnp.exp(-cum[i]) for i in n]
    e_rem = [jnp.exp(tot[i] - cum[i]) for i in n]
    at = [a[i] * jnp.exp(cum[i] - lw[i]) for i in n]
    rt = [r[i] * jnp.exp(cum[i]) for i in n]

    gram = [_dot_nt(jnp.concatenate([at[i], rt[i]], axis=0),
                    jnp.concatenate([bd(i, b[i] * e_inv[i]), bd(i, k[i] * e_inv[i])], axis=0)) for i in n]
    l_ab = [jnp.where(m[i]["strict"], gram[i][:C, :HW], 0.0) for i in n]
    l_ak = [jnp.where(m[i]["strict"], gram[i][:C, HW:], 0.0) for i in n]
    m_rb = [jnp.where(m[i]["incl"], gram[i][C:, :HW], 0.0) for i in n]
    m_rk = [jnp.where(m[i]["incl"], gram[i][C:, HW:], 0.0) for i in n]

    t_inv = [jnp.where(m[i]["eye"], 1.0, 0.0) + l_ab[i] for i in n]
    lp = [_dot(l_ab[i], bd(i, l_ab[i])) for i in n]
    nsq = int(math.log2(C)) - 1
    for j in range(1, nsq):
        res = [_dot(jnp.concatenate([lp[i], t_inv[i]], axis=0), bd(i, lp[i])) for i in n]
        lp = [res[i][:C] for i in n]
        t_inv = [t_inv[i] + res[i][C:] for i in n]
    t_inv = [t_inv[i] + _dot(t_inv[i], bd(i, lp[i])) for i in n]

    lv = [_dot(jnp.concatenate([l_ak[i], m_rk[i]], axis=0), bd(i, v[i])) for i in n]
    pq = [_dot(t_inv[i], jnp.concatenate([bd(i, at[i]), bd(i, lv[i][:C])], axis=1)) for i in n]
    mpq = [_dot(m_rb[i], jnp.concatenate([bd(i, pq[i][:, :HW]), bd(i, pq[i][:, HW:])], axis=1)) for i in n]
    rp = [rt[i] + mpq[i][:, :HW] for i in n]
    yq = [mpq[i][:, HW:] + lv[i][C:] for i in n]

    o1 = [_dot(jnp.concatenate([pq[i][:, :HW], rp[i]], axis=0), sbd[i]) for i in n]
    lhs_t = [jnp.transpose(jnp.concatenate([b[i] * e_rem[i], k[i] * e_rem[i]], axis=0)) for i in n]
    upd = [_dot(lhs_t[i], jnp.concatenate([o1[i][:C] + pq[i][:, HW:], v[i]], axis=0)) for i in n]
    wc_col = [jnp.sum(jnp.where(m[i]["diag"], jnp.exp(tot[i]), 0.0), axis=1, keepdims=True) for i in n]
    return [(o1[i][C:] + yq[i], wc_col[i] * sbd[i] + jnp.where(m[i]["bd"], upd[i], 0.0)) for i in n]


def _wkv_kernel(rf_ref, vf_ref, af_ref, lwf_ref, kdf_ref, bf_ref,
                rb_ref, vb_ref, ab_ref, lwb_ref, kdb_ref, bb_ref, yf_ref, yb_ref, s_ref, *, nb):
    @pl.when(pl.program_id(1) == 0)
    def _():
        s_ref[...] = jnp.zeros_like(s_ref)

    dirs = ((rf_ref, vf_ref, af_ref, lwf_ref, kdf_ref, bf_ref, yf_ref),
            (rb_ref, vb_ref, ab_ref, lwb_ref, kdb_ref, bb_ref, yb_ref))
    masks = (_wkv_masks(False), _wkv_masks(True))
    chains, dests = [], []
    for bi in range(nb):
        for d, (r_ref, v_ref, a_ref, lw_ref, kd_ref, b_ref, y_ref) in enumerate(dirs):
            for h in range(D_A // WKV_HALF):
                sl = slice(h * WKV_HALF, (h + 1) * WKV_HALF)
                chains.append((r_ref[bi, :, sl], a_ref[bi, :, sl], b_ref[bi, 0, :, sl], kd_ref[bi, 0, :, sl],
                               v_ref[bi, :, sl], lw_ref[bi, 0, :, sl], s_ref[bi, d, h], masks[d], d == 1))
                dests.append((y_ref, bi, sl, d, h))
    for (y, s_new), (y_ref, bi, sl, d, h) in zip(_wkv_chunks(chains), dests):
        y_ref[bi, :, sl] = y
        s_ref[bi, d, h] = s_new


def _wkv(r, v, na, lw, kd, bb, nb=2):
    b, s, _ = r.shape
    nc = _tiles(s, WKV_CHUNK)
    tok_f = pl.BlockSpec((nb, WKV_CHUNK, D_A), lambda bi, c: (bi, c, 0))
    tok_b = pl.BlockSpec((nb, WKV_CHUNK, D_A), lambda bi, c: (bi, nc - 1 - c, 0))
    dir_f = pl.BlockSpec((nb, 1, WKV_CHUNK, D_A), lambda bi, c: (bi, 0, c, 0))
    dir_b = pl.BlockSpec((nb, 1, WKV_CHUNK, D_A), lambda bi, c: (bi, 1, nc - 1 - c, 0))
    y_shape = jax.ShapeDtypeStruct((b, s, D_A), F32)
    return pl.pallas_call(
        functools.partial(_wkv_kernel, nb=nb),
        grid=(_tiles(b, nb), nc),
        in_specs=[tok_f] * 3 + [dir_f] * 3 + [tok_b] * 3 + [dir_b] * 3,
        out_specs=[tok_f, tok_b],
        out_shape=[y_shape, y_shape],
        scratch_shapes=[pltpu.VMEM((nb, 2, D_A // WKV_HALF, WKV_HALF, WKV_HALF), F32)],
        compiler_params=_params("parallel", "arbitrary"),
        name="wkv_scan",
    )(r, v, na, lw, kd, bb, r, v, na, lw, kd, bb)


def _conv_kernel(x_ref, xp_ref, xn_ref, w_ref, b_ref, c_ref, u_ref, *, sub):
    i = pl.program_id(1)
    n = pl.num_programs(1)
    ts = x_ref.shape[1]
    rows = ts + 2 * CONV_HALO
    u_ref[0, 0:CONV_HALO, :] = jnp.where(i > 0, xp_ref[0], 0.0)
    u_ref[0, CONV_HALO:CONV_HALO + ts, :] = x_ref[0]
    u_ref[0, CONV_HALO + ts:rows, :] = jnp.where(i < n - 1, xn_ref[0], 0.0)
    for s in range(1, SUBLANES):
        u_ref[s, 0:rows - SUBLANES, :] = u_ref[0, s:s + rows - SUBLANES, :]
    first = CONV_HALO - CONV_K // 2
    for t0 in range(0, ts, sub):
        acc = jnp.zeros((sub, D_B), F32) + b_ref[...]
        for j in range(CONV_K):
            q, s = divmod(first + j, SUBLANES)
            acc = acc + w_ref[j:j + 1, :] * u_ref[s, t0 + q * SUBLANES:t0 + q * SUBLANES + sub, :]
        c_ref[0, t0:t0 + sub, :] = acc


def _conv(u, lw, ts=256, sub=32):
    b, s, _ = u.shape
    nh = ts // CONV_HALO
    last = s // CONV_HALO - 1
    return pl.pallas_call(
        functools.partial(_conv_kernel, sub=sub),
        grid=(b, _tiles(s, ts)),
        in_specs=[pl.BlockSpec((1, ts, D_B), lambda bi, i: (bi, i, 0)),
                  pl.BlockSpec((1, CONV_HALO, D_B), lambda bi, i: (bi, jnp.maximum(i * nh - 1, 0), 0)),
                  pl.BlockSpec((1, CONV_HALO, D_B), lambda bi, i: (bi, jnp.minimum((i + 1) * nh, last), 0)),
                  pl.BlockSpec((CONV_K + 1, D_B), lambda bi, i: (0, 0)),
                  pl.BlockSpec((1, D_B), lambda bi, i: (0, 0))],
        out_specs=pl.BlockSpec((1, ts, D_B), lambda bi, i: (bi, i, 0)),
        out_shape=jax.ShapeDtypeStruct((b, s, D_B), F32),
        scratch_shapes=[pltpu.VMEM((SUBLANES, ts + 2 * CONV_HALO, D_B), F32)],
        compiler_params=_params("parallel", "parallel"),
        name="dwconv",
    )(u, u, u, lw["conv_w"], lw["conv_b"])


def _dft_kernel(x_ref, cw_ref, rev_ref, dm_ref, o_ref, rhs_ref, nyq_ref):
    i = pl.program_id(1)
    s = x_ref.shape[1]
    tm = o_ref.shape[1]
    half = s // 2
    nblk = s // DFT_BLK
    groups = [slice(g * GW_C, (g + 1) * GW_C) for g in range(D_C // GW_C)]

    @pl.when(i == 0)
    def _():
        for blk in range(half // DFT_BLK):
            rows = slice(blk * DFT_BLK, (blk + 1) * DFT_BLK)
            hi = x_ref[0, (nblk - blk - 1) * DFT_BLK:(nblk - blk) * DFT_BLK, :]
            if blk == 0:
                xr = jnp.dot(rev_ref[:, :DFT_BLK], hi, preferred_element_type=F32)
            else:
                top = x_ref[0, (nblk - blk) * DFT_BLK:(nblk - blk + 1) * DFT_BLK, :]
                xr = jnp.dot(rev_ref[...], jnp.concatenate([hi, top], axis=0), preferred_element_type=F32)
            xa = x_ref[0, rows, :].astype(F32)
            xe = (xa + xr).astype(BF16)
            xo = (xa - xr).astype(BF16)
            for sl in groups:
                rhs_ref[rows, sl] = jnp.dot(xe[:, sl], cw_ref[:, :GW_C], preferred_element_type=F32).astype(BF16)
                rhs_ref[half + blk * DFT_BLK:half + (blk + 1) * DFT_BLK, sl] = jnp.dot(
                    xo[:, sl], cw_ref[:, GW_C:], preferred_element_type=F32).astype(BF16)
        mid = x_ref[0, half:half + NYQ_ROWS, :]
        for sl in groups:
            nyq_ref[:, sl] = jnp.dot(mid[:, sl], cw_ref[:, :GW_C], preferred_element_type=F32) * s ** -0.5

    odd = (lax.broadcasted_iota(jnp.int32, (tm, D_C), 0) & 1) == 1
    nyq = nyq_ref[0:1, :]
    o_ref[0] = jnp.dot(dm_ref[...], rhs_ref[...], preferred_element_type=F32) + jnp.where(odd, -nyq, nyq)


def _dft(zc, cw, rev, dm, tm=512):
    b, s, _ = zc.shape
    assert tm % 2 == 0 and (s // 2) % DFT_BLK == 0
    return pl.pallas_call(
        _dft_kernel,
        grid=(b, _tiles(s, tm)),
        in_specs=[pl.BlockSpec((1, s, D_C), lambda bi, i: (bi, 0, 0)),
                  pl.BlockSpec(cw.shape, lambda bi, i: (0, 0)),
                  pl.BlockSpec(rev.shape, lambda bi, i: (0, 0)),
                  pl.BlockSpec((tm, s), lambda bi, i: (i, 0))],
        out_specs=pl.BlockSpec((1, tm, D_C), lambda bi, i: (bi, i, 0)),
        out_shape=jax.ShapeDtypeStruct((b, s, D_C), F32),
        scratch_shapes=[pltpu.VMEM((s, D_C), BF16), pltpu.VMEM((NYQ_ROWS, D_C), F32)],
        compiler_params=_params("parallel", "arbitrary"),
        name="dft2_real",
    )(zc, cw, rev, dm)


def _dft_tables(s):
    def cs(n, cols):
        ang = ((jnp.arange(n, dtype=jnp.int32)[:, None] * jnp.arange(cols, dtype=jnp.int32)[None, :]) % n
               ).astype(F32) * (2.0 * math.pi / n)
        return jnp.cos(ang), jnp.sin(ang)
    cc, sc = cs(GW_C, GW_C)
    cw = (jnp.concatenate([cc, sc], axis=1) * GW_C ** -0.5).astype(BF16)
    j = jnp.arange(DFT_BLK, dtype=jnp.int32)[:, None]
    c = jnp.arange(2 * DFT_BLK, dtype=jnp.int32)[None, :]
    rev = (((j >= 1) & (c == DFT_BLK - j)) | ((j == 0) & (c == DFT_BLK))).astype(BF16)
    cp, sp = cs(s, s // 2)
    dm = (jnp.concatenate([cp, -sp], axis=1) * s ** -0.5).astype(BF16)
    return cw, rev, dm


def _merge_kernel(x_ref, yf_ref, yb_ref, bon_ref, g_ref, c_ref, f_ref, gt_ref,
                  lnxg_ref, lnxb_ref, clng_ref, clnb_ref, bdm_ref,
                  wa_ref, wb_ref, wc_ref, wo_ref, o_ref):
    bdm = bdm_ref[...]
    ys = yf_ref[0] + yb_ref[0]
    dev = ys - _split_dot(ys, bdm)
    var = _dot(dev * dev, bdm)
    yn = dev * lax.rsqrt(var + EPS_GN) * lnxg_ref[...] + lnxb_ref[...]
    ya = _dot((yn + bon_ref[0]) * g_ref[0], wa_ref[...])

    c = c_ref[0]
    cd = c - jnp.mean(c, axis=-1, keepdims=True)
    cn = cd * lax.rsqrt(jnp.mean(cd * cd, axis=-1, keepdims=True) + EPS_LN) * clng_ref[...] + clnb_ref[...]
    yb = _dot(cn * jax.nn.sigmoid(cn), wb_ref[...])

    yc = _dot(f_ref[0], wc_ref[...])

    gate = lambda j: gt_ref[0, :, j * D_MODEL:(j + 1) * D_MODEL].astype(F32)
    merged = gate(0) * ya + gate(1) * yb + gate(2) * yc
    o_ref[0] = x_ref[0] + _dot(merged, wo_ref[...])


def _merge(x, yf, yb, bon, g, c, f, gates, lw, tm=256):
    b, s, _ = x.shape
    full = lambda a: pl.BlockSpec(a.shape, lambda bi, i: (0,) * a.ndim)
    tok = lambda n: pl.BlockSpec((1, tm, n), lambda bi, i: (bi, i, 0))
    consts = (lw["lnx_g"], lw["lnx_b"], lw["conv_ln_g"], lw["conv_ln_b"], lw["bd_mean"],
              lw["w_a_out"], lw["w_b_out"], lw["w_c_out"], lw["w_out"])
    return pl.pallas_call(
        _merge_kernel,
        grid=(b, _tiles(s, tm)),
        in_specs=[tok(D_MODEL), tok(D_A), tok(D_A), tok(D_A), tok(D_A), tok(D_B), tok(D_C), tok(3 * D_MODEL)] + [full(a) for a in consts],
        out_specs=tok(D_MODEL),
        out_shape=jax.ShapeDtypeStruct(x.shape, F32),
        compiler_params=_params("parallel", "parallel"),
        name="merge",
    )(x, yf, yb, bon, g, c, f, gates, *consts)


def _norm_proj_kernel(x_ref, g_ref, w_ref, o_ref):
    o_ref[...] = _dot(_rms(x_ref[...], g_ref[...]), w_ref[...]).astype(o_ref.dtype)


def _norm_proj(x2, g, w, out_dtype, tm=256):
    t = x2.shape[0]
    n = w.shape[1]
    return pl.pallas_call(
        _norm_proj_kernel,
        grid=(_tiles(t, tm),),
        in_specs=[pl.BlockSpec((tm, D_MODEL), lambda i: (i, 0)),
                  pl.BlockSpec((1, D_MODEL), lambda i: (0, 0)),
                  pl.BlockSpec(w.shape, lambda i: (0, 0))],
        out_specs=pl.BlockSpec((tm, n), lambda i: (i, 0)),
        out_shape=jax.ShapeDtypeStruct((t, n), out_dtype),
        compiler_params=_params("parallel"),
        name="norm_proj",
    )(x2, g, w)


def _xattn_kernel(x_ref, kv_ref, g_ref, wq_ref, wo_ref, o_ref):
    x = x_ref[0]
    q = (_dot(_rms(x, g_ref[...]), wq_ref[...]) * HD_X ** -0.5).astype(BF16)
    heads = []
    for h in range(H_X):
        sl = slice(h * HD_X, (h + 1) * HD_X)
        sc = _dot_nt(q[:, sl], kv_ref[0, :, sl])
        e = jnp.exp(sc - jnp.max(sc, axis=-1, keepdims=True))
        p = e / jnp.sum(e, axis=-1, keepdims=True)
        heads.append(_dot(p, kv_ref[0, :, D_MODEL + h * HD_X:D_MODEL + (h + 1) * HD_X]))
    o_ref[0] = x + _dot(jnp.concatenate(heads, axis=1), wo_ref[...])


def _xattn(x, kv, lw, tm=512):
    b, s, _ = x.shape
    full = lambda a: pl.BlockSpec(a.shape, lambda bi, i: (0,) * a.ndim)
    return pl.pallas_call(
        _xattn_kernel,
        grid=(b, _tiles(s, tm)),
        in_specs=[pl.BlockSpec((1, tm, D_MODEL), lambda bi, i: (bi, i, 0)),
                  pl.BlockSpec((1, N_MEM, 2 * D_MODEL), lambda bi, i: (bi, 0, 0)),
                  full(lw["norm_x_g"]), full(lw["wq"]), full(lw["wo"])],
        out_specs=pl.BlockSpec((1, tm, D_MODEL), lambda bi, i: (bi, i, 0)),
        out_shape=jax.ShapeDtypeStruct(x.shape, F32),
        compiler_params=_params("parallel", "parallel"),
        name="cross_attn",
    )(x, kv, lw["norm_x_g"], lw["wq"], lw["wo"])


def _mlp_kernel(x_ref, g_ref, w1_ref, w2_ref, fg_ref, o_ref, hn_ref, acc_ref, *, final):
    j = pl.program_id(1)

    @pl.when(j == 0)
    def _():
        hn_ref[...] = _rms(x_ref[...], g_ref[...]).astype(BF16)
        acc_ref[...] = jnp.zeros_like(acc_ref)

    h1 = jnp.maximum(jnp.dot(hn_ref[...], w1_ref[...], preferred_element_type=F32), 0.0)
    acc_ref[...] += _dot(h1 * h1, w2_ref[...])

    @pl.when(j == pl.num_programs(1) - 1)
    def _():
        y = x_ref[...] + acc_ref[...]
        o_ref[...] = _rms(y, fg_ref[...]) if final else y


def _mlp(x2, lw, final_g, final, tm=1024, tf=1024):
    t = x2.shape[0]
    return pl.pallas_call(
        functools.partial(_mlp_kernel, final=final),
        grid=(_tiles(t, tm), _tiles(D_FF, tf)),
        in_specs=[pl.BlockSpec((tm, D_MODEL), lambda i, j: (i, 0)),
                  pl.BlockSpec((1, D_MODEL), lambda i, j: (0, 0)),
                  pl.BlockSpec((D_MODEL, tf), lambda i, j: (0, j)),
                  pl.BlockSpec((tf, D_MODEL), lambda i, j: (j, 0)),
                  pl.BlockSpec((1, D_MODEL), lambda i, j: (0, 0))],
        out_specs=pl.BlockSpec((tm, D_MODEL), lambda i, j: (i, 0)),
        out_shape=jax.ShapeDtypeStruct((t, D_MODEL), F32),
        scratch_shapes=[pltpu.VMEM((tm, D_MODEL), BF16), pltpu.VMEM((tm, D_MODEL), F32)],
        compiler_params=_params("parallel", "arbitrary"),
        name="mlp",
    )(x2, lw["norm_mlp_g"], lw["w_mlp1"], lw["w_mlp2"], final_g)


def _stage_layer(p, l):
    row = lambda a: a.reshape(1, -1).astype(F32)
    pad_lora = lambda w: jnp.stack([
        jnp.zeros((2 * LORA, D_A), F32).at[d * LORA:(d + 1) * LORA].set(w[d]) for d in range(2)]).astype(BF16)
    hid = jnp.arange(D_A, dtype=jnp.int32) // HEAD_A
    same_head = (hid[:, None] == hid[None, :]).astype(F32)
    return dict(
        norm_mix_g=row(p["norm_mix_g"][l]), w_in=p["w_in"][l].astype(BF16),
        mu=p["shift_mu"][l], w0=p["w0"][l], wup=pad_lora(p["w_up"][l]), a0=p["a0"][l],
        aup=pad_lora(p["a_up"][l]), gup=p["g_up"][l].astype(BF16),
        k_k=row(p["k_k"][l]), k_a=row(p["k_a"][l]), r_k=row(p["r_k"][l]),
        lnx_g=row(p["lnx_g"][l]), lnx_b=row(p["lnx_b"][l]),
        bd_ones=same_head.astype(BF16), bd_mean=(same_head / HEAD_A).astype(BF16),
        conv_w=jnp.concatenate([p["conv_w"][l], jnp.zeros((1, D_B), F32)], axis=0), conv_b=row(p["conv_b"][l]),
        conv_ln_g=row(p["conv_ln_g"][l]), conv_ln_b=row(p["conv_ln_b"][l]),
        gate_b=row(p["gate_b"][l]),
        w_a_out=p["w_a_out"][l].astype(BF16), w_b_out=p["w_b_out"][l].astype(BF16),
        w_c_out=p["w_c_out"][l].astype(BF16), w_out=p["w_out"][l].astype(BF16),
        norm_x_g=row(p["norm_x_g"][l]), norm_mem_g=row(p["norm_mem_g"][l]),
        wq=p["wq"][l].astype(BF16), wo=p["wo"][l].astype(BF16),
        wkv=jnp.concatenate([p["wk"][l], p["wv"][l]], axis=1).astype(BF16),
        norm_mlp_g=row(p["norm_mlp_g"][l]),
        w_mlp1=p["w_mlp1"][l].astype(BF16), w_mlp2=p["w_mlp2"][l].astype(BF16),
    )


def _trunk(x, mem, layers, final_g):
    b, s, _ = x.shape
    t = b * s
    cw, rev, dm = _dft_tables(s)
    mem2 = mem.reshape(b * N_MEM, D_MODEL)
    for l, lw in enumerate(layers):
        za, u, zc, gates = _in_proj(x.reshape(t, D_MODEL), lw["norm_mix_g"], lw["w_in"], lw["gate_b"])
        r, v, na, g, bon, lwd, kd, bb = _rwkv_prep(za.reshape(b, s, N_A_IN), lw)
        yf, yb = _wkv(r, v, na, lwd, kd, bb)
        c = _conv(u.reshape(b, s, D_B), lw)
        f = _dft(zc.reshape(b, s, D_C), cw, rev, dm)
        x = _merge(x, yf, yb, bon, g, c, f, gates.reshape(b, s, 3 * D_MODEL), lw)
        kv = _norm_proj(mem2, lw["norm_mem_g"], lw["wkv"], BF16).reshape(b, N_MEM, 2 * D_MODEL)
        x = _xattn(x, kv, lw)
        x = _mlp(x.reshape(t, D_MODEL), lw, final_g, l == len(layers) - 1).reshape(b, s, D_MODEL)
    return x


def kernel(x_prompt, x_sample, mem_prompt, mem_sample, norm_mix_g, w_in, shift_mu, w0, w_up, a0, a_up, g_up, k_k, k_a, r_k, lnx_g, lnx_b, w_a_out, conv_w, conv_b, conv_ln_g, conv_ln_b, w_b_out, w_c_out, gate_b, w_out, norm_x_g, norm_mem_g, wq, wk, wv, wo, norm_mlp_g, w_mlp1, w_mlp2, final_norm_g):
    p = dict(norm_mix_g=norm_mix_g, w_in=w_in, shift_mu=shift_mu, w0=w0, w_up=w_up, a0=a0,
             a_up=a_up, g_up=g_up, k_k=k_k, k_a=k_a, r_k=r_k, lnx_g=lnx_g, lnx_b=lnx_b,
             w_a_out=w_a_out, conv_w=conv_w, conv_b=conv_b, conv_ln_g=conv_ln_g,
             conv_ln_b=conv_ln_b, w_b_out=w_b_out, w_c_out=w_c_out, gate_b=gate_b, w_out=w_out,
             norm_x_g=norm_x_g, norm_mem_g=norm_mem_g, wq=wq, wk=wk, wv=wv, wo=wo,
             norm_mlp_g=norm_mlp_g, w_mlp1=w_mlp1, w_mlp2=w_mlp2)
    layers = [_stage_layer(p, l) for l in range(DEPTH)]
    final_g = final_norm_g.reshape(1, D_MODEL).astype(F32)
    return (_trunk(x_prompt, mem_prompt, layers, final_g), _trunk(x_sample, mem_sample, layers, final_g))
```

```python
import functools
import math

import jax
import jax.numpy as jnp
from jax import lax
from jax.experimental import pallas as pl
from jax.experimental.pallas import tpu as pltpu

F32 = jnp.float32
BF16 = jnp.bfloat16

D_MODEL = 1024
DEPTH = 4
N_MEM = 256
D_A = 512
HEAD_A = 64
LORA = 64
LORA_G = 128
N_A_IN = 3 * D_A + 4 * LORA + LORA_G
D_B = 512
CONV_K = 31
D_C = 512
GW_C = 128
H_X = 4
HD_X = D_MODEL // H_X
D_FF = 4 * D_MODEL

EPS_RMS = 1e-6
EPS_LN = 1e-5
EPS_GN = 64e-5

VMEM_LIMIT_BYTES = 56 * 1024 * 1024

WKV_CHUNK = 64
WKV_HALF = 128
CONV_HALO = 16
IN_HALO = 16
GATE_CHUNK = 512
SUBLANES = 8
DFT_BLK = 256
NYQ_ROWS = 16


def _params(*sem):
    return pltpu.CompilerParams(dimension_semantics=sem, vmem_limit_bytes=VMEM_LIMIT_BYTES)


def _tiles(n, tile):
    assert n % tile == 0, (n, tile)
    return n // tile


def _dot(a, b):
    return jnp.dot(a.astype(BF16), b.astype(BF16), preferred_element_type=F32)


def _dot_nt(a, b):
    return lax.dot_general(a.astype(BF16), b.astype(BF16), (((1,), (1,)), ((), ())),
                           preferred_element_type=F32)


def _split_dot(x, m):
    xh = x.astype(BF16)
    xl = (x - xh.astype(F32)).astype(BF16)
    return jnp.dot(xh, m, preferred_element_type=F32) + jnp.dot(xl, m, preferred_element_type=F32)


def _split_dot_lhs(m, x):
    xh = x.astype(BF16)
    xl = (x - xh.astype(F32)).astype(BF16)
    return jnp.dot(m, xh, preferred_element_type=F32) + jnp.dot(m, xl, preferred_element_type=F32)


def _rms(x, g):
    return x * lax.rsqrt(jnp.mean(x * x, axis=-1, keepdims=True) + EPS_RMS) * g


def _in_proj_kernel(x_ref, xp_ref, xn_ref, g_ref, w_ref, gb_ref, mu_ref, w0_ref, wup_ref, a0_ref, aup_ref,
                    gup_ref, kk_ref, ka_ref, rk_ref, bd_ref,
                    u_ref, zc_ref, gt_ref, r_ref, v_ref, na_ref, g_out_ref, bon_ref, lw_ref, kd_ref, b_ref,
                    za_ref):
    i = pl.program_id(1)
    n = pl.num_programs(1)
    tm = x_ref.shape[1]
    hn = _rms(x_ref[0], g_ref[...]).astype(BF16)
    hp = jnp.where(i > 0, _rms(xp_ref[0], g_ref[...]), 0.0).astype(BF16)
    hx = jnp.where(i < n - 1, _rms(xn_ref[0], g_ref[...]), 0.0).astype(BF16)
    proj = lambda h, lo, width: jnp.dot(h, w_ref[:, lo:lo + width], preferred_element_type=F32)

    gate_lo = N_A_IN + 2 * D_B + D_C

    def gate_chunk(j):
        cols = slice(j * GATE_CHUNK, (j + 1) * GATE_CHUNK)
        gt_ref[0, :, cols] = jax.nn.sigmoid(
            proj(hn, gate_lo + j * GATE_CHUNK, GATE_CHUNK) + gb_ref[:, cols]).astype(BF16)

    zb = proj(hn, N_A_IN, 2 * D_B)
    za_ref[...] = proj(jnp.concatenate([hp, hn, hx], axis=0), 0, N_A_IN)
    u_ref[0] = zb[:, :D_B] * jax.nn.sigmoid(zb[:, D_B:])

    gate_chunk(0)
    z = za_ref[IN_HALO:IN_HALO + tm, :]
    zprev = za_ref[IN_HALO - 1:IN_HALO - 1 + tm, :]
    znext = za_ref[IN_HALO + 1:IN_HALO + 1 + tm, :]
    zs = z + mu_ref[0:1, :] * (zprev - z) + mu_ref[1:2, :] * (znext - z)
    r = zs[:, 0:D_A]
    k = zs[:, D_A:2 * D_A]
    v = zs[:, 2 * D_A:3 * D_A]
    o = 3 * D_A
    tdw = jnp.tanh(zs[:, o:o + 2 * LORA]).astype(BF16)
    dab = zs[:, o + 2 * LORA:o + 4 * LORA].astype(BF16)
    dg = zs[:, o + 4 * LORA:o + 4 * LORA + LORA_G]
    bd = bd_ref[...]
    r_ref[0] = r
    v_ref[0] = v

    gate_chunk(1)
    g_out_ref[0] = _dot(jax.nn.sigmoid(dg), gup_ref[...])
    kk0 = k * kk_ref[...]
    nrm = jnp.sqrt(_dot(kk0 * kk0, bd))
    kk = kk0 / jnp.maximum(nrm, 1e-12)
    na_ref[0] = -kk
    ksum = jnp.zeros_like(k)
    for d in range(2):
        gate_chunk(2 + 2 * d)
        wl = w0_ref[d:d + 1, :] + jnp.dot(tdw, wup_ref[d], preferred_element_type=F32)
        lw_ref[0, d] = (-math.exp(-0.5)) * jax.nn.sigmoid(wl)
        a = jax.nn.sigmoid(a0_ref[d:d + 1, :] + jnp.dot(dab, aup_ref[d], preferred_element_type=F32))
        gate_chunk(3 + 2 * d)
        kd = k * (1.0 + (a - 1.0) * ka_ref[...])
        kd_ref[0, d] = kd
        b_ref[0, d] = kk * a
        ksum = ksum + kd
    zc_ref[0] = proj(hn, N_A_IN + 2 * D_B, D_C).astype(BF16)
    bon_ref[0] = _dot(r * rk_ref[...] * ksum, bd) * v


def _in_proj(x, lw, tm=256):
    b, s, _ = x.shape
    nh = tm // IN_HALO
    last = s // IN_HALO - 1
    const = lambda a: pl.BlockSpec(a.shape, lambda bi, i: (0,) * a.ndim, pipeline_mode=pl.Buffered(1))
    consts = (lw["norm_mix_g"], lw["w_in"], lw["gate_b"], lw["mu"], lw["w0"], lw["wup"], lw["a0"], lw["aup"],
              lw["gup"], lw["k_k"], lw["k_a"], lw["r_k"], lw["bd_ones"])
    tok = lambda width: pl.BlockSpec((1, tm, width), lambda bi, i: (bi, i, 0))
    dirs = pl.BlockSpec((1, 2, tm, D_A), lambda bi, i: (bi, 0, i, 0))
    tok_shape = lambda width, dt: jax.ShapeDtypeStruct((b, s, width), dt)
    dir_shape = jax.ShapeDtypeStruct((b, 2, s, D_A), F32)
    return pl.pallas_call(
        _in_proj_kernel,
        grid=(b, _tiles(s, tm)),
        in_specs=[tok(D_MODEL),
                  pl.BlockSpec((1, IN_HALO, D_MODEL), lambda bi, i: (bi, jnp.maximum(i * nh - 1, 0), 0)),
                  pl.BlockSpec((1, IN_HALO, D_MODEL), lambda bi, i: (bi, jnp.minimum((i + 1) * nh, last), 0)),
                  ] + [const(c) for c in consts],
        out_specs=[tok(D_B), tok(D_C), tok(3 * D_MODEL)] + [tok(D_A)] * 5 + [dirs] * 3,
        out_shape=[tok_shape(D_B, F32), tok_shape(D_C, BF16), tok_shape(3 * D_MODEL, BF16)]
        + [tok_shape(D_A, F32)] * 5 + [dir_shape] * 3,
        scratch_shapes=[pltpu.VMEM((tm + 2 * IN_HALO, N_A_IN), F32)],
        compiler_params=_params("parallel", "parallel"),
        name="in_proj",
    )(x, x, x, *consts)


def _wkv_masks(reverse):
    C, HW = WKV_CHUNK, WKV_HALF
    row = lax.broadcasted_iota(jnp.int32, (C, HW), 0)
    lane_i = lax.broadcasted_iota(jnp.int32, (C, HW), 1) & (C - 1)
    dt = (lane_i - row) if reverse else (row - lane_i)
    ti = lax.broadcasted_iota(jnp.int32, (C, C), 0)
    tj = lax.broadcasted_iota(jnp.int32, (C, C), 1)
    rr = lax.broadcasted_iota(jnp.int32, (HW, HW), 0)
    cc = lax.broadcasted_iota(jnp.int32, (HW, HW), 1)
    return dict(tri=jnp.where((tj >= ti) if reverse else (tj <= ti), 1.0, 0.0).astype(BF16),
                strict=dt > 0, incl=dt >= 0, eye=dt == 0,
                bd=(rr >> 6) == (cc >> 6), diag=rr == cc)


def _wkv_chunks(chains):
    C, HW = WKV_CHUNK, WKV_HALF
    n = range(len(chains))
    r, a, b, k, v, lw, sbd, m, rev = zip(*chains)
    bd = lambda i, x: jnp.where(m[i]["bd"], jnp.concatenate([x] * (HW // C), axis=0), 0.0)

    cum = [_split_dot_lhs(m[i]["tri"], lw[i]) for i in n]
    tot = [cum[i][0:1, :] if rev[i] else cum[i][C - 1:C, :] for i in n]
    e_inv = [jnp.exp(-cum[i]) for i in n]
    e_rem = [jnp.exp(tot[i] - cum[i]) for i in n]
    at = [a[i] * jnp.exp(cum[i] - lw[i]) for i in n]
    rt = [r[i] * jnp.exp(cum[i]) for i in n]

    gram = [_dot_nt(jnp.concatenate([at[i], rt[i]], axis=0),
                    jnp.concatenate([bd(i, b[i] * e_inv[i]), bd(i, k[i] * e_inv[i])], axis=0)) for i in n]
    l_ab = [jnp.where(m[i]["strict"], gram[i][:C, :HW], 0.0) for i in n]
    l_ak = [jnp.where(m[i]["strict"], gram[i][:C, HW:], 0.0) for i in n]
    m_rb = [jnp.where(m[i]["incl"], gram[i][C:, :HW], 0.0) for i in n]
    m_rk = [jnp.where(m[i]["incl"], gram[i][C:, HW:], 0.0) for i in n]

    t_inv = [jnp.where(m[i]["eye"], 1.0, 0.0) + l_ab[i] for i in n]
    lp = [_dot(l_ab[i], bd(i, l_ab[i])) for i in n]
    nsq = int(math.log2(C)) - 1
    for j in range(1, nsq):
        res = [_dot(jnp.concatenate([lp[i], t_inv[i]], axis=0), bd(i, lp[i])) for i in n]
        lp = [res[i][:C] for i in n]
        t_inv = [t_inv[i] + res[i][C:] for i in n]
    t_inv = [t_inv[i] + _dot(t_inv[i], bd(i, lp[i])) for i in n]

    lv = [_dot(jnp.concatenate([l_ak[i], m_rk[i]], axis=0), bd(i, v[i])) for i in n]
    pq = [_dot(t_inv[i], jnp.concatenate([bd(i, at[i]), bd(i, lv[i][:C])], axis=1)) for i in n]
    mpq = [_dot(m_rb[i], jnp.concatenate([bd(i, pq[i][:, :HW]), bd(i, pq[i][:, HW:])], axis=1)) for i in n]
    rp = [rt[i] + mpq[i][:, :HW] for i in n]
    yq = [mpq[i][:, HW:] + lv[i][C:] for i in n]

    o1 = [_dot(jnp.concatenate([pq[i][:, :HW], rp[i]], axis=0), sbd[i]) for i in n]
    lhs_t = [jnp.transpose(jnp.concatenate([b[i] * e_rem[i], k[i] * e_rem[i]], axis=0)) for i in n]
    upd = [_dot(lhs_t[i], jnp.concatenate([o1[i][:C] + pq[i][:, HW:], v[i]], axis=0)) for i in n]
    wc_col = [jnp.sum(jnp.where(m[i]["diag"], jnp.exp(tot[i]), 0.0), axis=1, keepdims=True) for i in n]
    return [(o1[i][C:] + yq[i], wc_col[i] * sbd[i] + jnp.where(m[i]["bd"], upd[i], 0.0)) for i in n]


def _wkv_kernel(rf_ref, vf_ref, af_ref, lwf_ref, kdf_ref, bf_ref,
                rb_ref, vb_ref, ab_ref, lwb_ref, kdb_ref, bb_ref, yf_ref, yb_ref, s_ref, *, nb):
    @pl.when(pl.program_id(1) == 0)
    def _():
        s_ref[...] = jnp.zeros_like(s_ref)

    dirs = ((rf_ref, vf_ref, af_ref, lwf_ref, kdf_ref, bf_ref, yf_ref),
            (rb_ref, vb_ref, ab_ref, lwb_ref, kdb_ref, bb_ref, yb_ref))
    masks = (_wkv_masks(False), _wkv_masks(True))
    chains, dests = [], []
    for bi in range(nb):
        for d, (r_ref, v_ref, a_ref, lw_ref, kd_ref, b_ref, y_ref) in enumerate(dirs):
            for h in range(D_A // WKV_HALF):
                sl = slice(h * WKV_HALF, (h + 1) * WKV_HALF)
                chains.append((r_ref[bi, :, sl], a_ref[bi, :, sl], b_ref[bi, 0, :, sl], kd_ref[bi, 0, :, sl],
                               v_ref[bi, :, sl], lw_ref[bi, 0, :, sl], s_ref[bi, d, h], masks[d], d == 1))
                dests.append((y_ref, bi, sl, d, h))
    for (y, s_new), (y_ref, bi, sl, d, h) in zip(_wkv_chunks(chains), dests):
        y_ref[bi, :, sl] = y
        s_ref[bi, d, h] = s_new


def _wkv(r, v, na, lw, kd, bb, nb=2):
    b, s, _ = r.shape
    nc = _tiles(s, WKV_CHUNK)
    tok_f = pl.BlockSpec((nb, WKV_CHUNK, D_A), lambda bi, c: (bi, c, 0))
    tok_b = pl.BlockSpec((nb, WKV_CHUNK, D_A), lambda bi, c: (bi, nc - 1 - c, 0))
    dir_f = pl.BlockSpec((nb, 1, WKV_CHUNK, D_A), lambda bi, c: (bi, 0, c, 0))
    dir_b = pl.BlockSpec((nb, 1, WKV_CHUNK, D_A), lambda bi, c: (bi, 1, nc - 1 - c, 0))
    y_shape = jax.ShapeDtypeStruct((b, s, D_A), F32)
    return pl.pallas_call(
        functools.partial(_wkv_kernel, nb=nb),
        grid=(_tiles(b, nb), nc),
        in_specs=[tok_f] * 3 + [dir_f] * 3 + [tok_b] * 3 + [dir_b] * 3,
        out_specs=[tok_f, tok_b],
        out_shape=[y_shape, y_shape],
        scratch_shapes=[pltpu.VMEM((nb, 2, D_A // WKV_HALF, WKV_HALF, WKV_HALF), F32)],
        compiler_params=_params("parallel", "arbitrary"),
        name="wkv_scan",
    )(r, v, na, lw, kd, bb, r, v, na, lw, kd, bb)


def _conv_kernel(x_ref, xp_ref, xn_ref, w_ref, b_ref, c_ref, u_ref, *, sub):
    i = pl.program_id(1)
    n = pl.num_programs(1)
    ts = x_ref.shape[1]
    rows = ts + 2 * CONV_HALO
    u_ref[0, 0:CONV_HALO, :] = jnp.where(i > 0, xp_ref[0], 0.0)
    u_ref[0, CONV_HALO:CONV_HALO + ts, :] = x_ref[0]
    u_ref[0, CONV_HALO + ts:rows, :] = jnp.where(i < n - 1, xn_ref[0], 0.0)
    for s in range(1, SUBLANES):
        u_ref[s, 0:rows - SUBLANES, :] = u_ref[0, s:s + rows - SUBLANES, :]
    first = CONV_HALO - CONV_K // 2
    for t0 in range(0, ts, sub):
        acc = jnp.zeros((sub, D_B), F32) + b_ref[...]
        for j in range(CONV_K):
            q, s = divmod(first + j, SUBLANES)
            acc = acc + w_ref[j:j + 1, :] * u_ref[s, t0 + q * SUBLANES:t0 + q * SUBLANES + sub, :]
        c_ref[0, t0:t0 + sub, :] = acc


def _conv(u, lw, ts=256, sub=32):
    b, s, _ = u.shape
    nh = ts // CONV_HALO
    last = s // CONV_HALO - 1
    return pl.pallas_call(
        functools.partial(_conv_kernel, sub=sub),
        grid=(b, _tiles(s, ts)),
        in_specs=[pl.BlockSpec((1, ts, D_B), lambda bi, i: (bi, i, 0)),
                  pl.BlockSpec((1, CONV_HALO, D_B), lambda bi, i: (bi, jnp.maximum(i * nh - 1, 0), 0)),
                  pl.BlockSpec((1, CONV_HALO, D_B), lambda bi, i: (bi, jnp.minimum((i + 1) * nh, last), 0)),
                  pl.BlockSpec((CONV_K + 1, D_B), lambda bi, i: (0, 0)),
                  pl.BlockSpec((1, D_B), lambda bi, i: (0, 0))],
        out_specs=pl.BlockSpec((1, ts, D_B), lambda bi, i: (bi, i, 0)),
        out_shape=jax.ShapeDtypeStruct((b, s, D_B), F32),
        scratch_shapes=[pltpu.VMEM((SUBLANES, ts + 2 * CONV_HALO, D_B), F32)],
        compiler_params=_params("parallel", "parallel"),
        name="dwconv",
    )(u, u, u, lw["conv_w"], lw["conv_b"])


def _dft_kernel(x_ref, cw_ref, rev_ref, dm_ref, o_ref, rhs_ref, nyq_ref):
    i = pl.program_id(1)
    s = x_ref.shape[1]
    tm = o_ref.shape[1]
    half = s // 2
    nblk = s // DFT_BLK
    groups = [slice(g * GW_C, (g + 1) * GW_C) for g in range(D_C // GW_C)]

    @pl.when(i == 0)
    def _():
        for blk in range(half // DFT_BLK):
            rows = slice(blk * DFT_BLK, (blk + 1) * DFT_BLK)
            hi = x_ref[0, (nblk - blk - 1) * DFT_BLK:(nblk - blk) * DFT_BLK, :]
            if blk == 0:
                xr = jnp.dot(rev_ref[:, :DFT_BLK], hi, preferred_element_type=F32)
            else:
                top = x_ref[0, (nblk - blk) * DFT_BLK:(nblk - blk + 1) * DFT_BLK, :]
                xr = jnp.dot(rev_ref[...], jnp.concatenate([hi, top], axis=0), preferred_element_type=F32)
            xa = x_ref[0, rows, :].astype(F32)
            xe = (xa + xr).astype(BF16)
            xo = (xa - xr).astype(BF16)
            for sl in groups:
                rhs_ref[rows, sl] = jnp.dot(xe[:, sl], cw_ref[:, :GW_C], preferred_element_type=F32).astype(BF16)
                rhs_ref[half + blk * DFT_BLK:half + (blk + 1) * DFT_BLK, sl] = jnp.dot(
                    xo[:, sl], cw_ref[:, GW_C:], preferred_element_type=F32).astype(BF16)
        mid = x_ref[0, half:half + NYQ_ROWS, :]
        for sl in groups:
            nyq_ref[:, sl] = jnp.dot(mid[:, sl], cw_ref[:, :GW_C], preferred_element_type=F32) * s ** -0.5

    odd = (lax.broadcasted_iota(jnp.int32, (tm, D_C), 0) & 1) == 1
    nyq = nyq_ref[0:1, :]
    o_ref[0] = jnp.dot(dm_ref[...], rhs_ref[...], preferred_element_type=F32) + jnp.where(odd, -nyq, nyq)


def _dft(zc, cw, rev, dm, tm=512):
    b, s, _ = zc.shape
    assert tm % 2 == 0 and (s // 2) % DFT_BLK == 0
    return pl.pallas_call(
        _dft_kernel,
        grid=(b, _tiles(s, tm)),
        in_specs=[pl.BlockSpec((1, s, D_C), lambda bi, i: (bi, 0, 0)),
                  pl.BlockSpec(cw.shape, lambda bi, i: (0, 0)),
                  pl.BlockSpec(rev.shape, lambda bi, i: (0, 0)),
                  pl.BlockSpec((tm, s), lambda bi, i: (i, 0))],
        out_specs=pl.BlockSpec((1, tm, D_C), lambda bi, i: (bi, i, 0)),
        out_shape=jax.ShapeDtypeStruct((b, s, D_C), F32),
        scratch_shapes=[pltpu.VMEM((s, D_C), BF16), pltpu.VMEM((NYQ_ROWS, D_C), F32)],
        compiler_params=_params("parallel", "arbitrary"),
        name="dft2_real",
    )(zc, cw, rev, dm)


def _dft_tables(s):
    def cs(n, cols):
        ang = ((jnp.arange(n, dtype=jnp.int32)[:, None] * jnp.arange(cols, dtype=jnp.int32)[None, :]) % n
               ).astype(F32) * (2.0 * math.pi / n)
        return jnp.cos(ang), jnp.sin(ang)
    cc, sc = cs(GW_C, GW_C)
    cw = (jnp.concatenate([cc, sc], axis=1) * GW_C ** -0.5).astype(BF16)
    j = jnp.arange(DFT_BLK, dtype=jnp.int32)[:, None]
    c = jnp.arange(2 * DFT_BLK, dtype=jnp.int32)[None, :]
    rev = (((j >= 1) & (c == DFT_BLK - j)) | ((j == 0) & (c == DFT_BLK))).astype(BF16)
    cp, sp = cs(s, s // 2)
    dm = (jnp.concatenate([cp, -sp], axis=1) * s ** -0.5).astype(BF16)
    return cw, rev, dm


def _merge_kernel(x_ref, yf_ref, yb_ref, bon_ref, g_ref, c_ref, f_ref, gt_ref,
                  lnxg_ref, lnxb_ref, clng_ref, clnb_ref, bdm_ref,
                  wa_ref, wb_ref, wc_ref, wo_ref, o_ref):
    bdm = bdm_ref[...]
    ys = yf_ref[0] + yb_ref[0]
    dev = ys - _split_dot(ys, bdm)
    var = _dot(dev * dev, bdm)
    yn = dev * lax.rsqrt(var + EPS_GN) * lnxg_ref[...] + lnxb_ref[...]
    ya = _dot((yn + bon_ref[0]) * g_ref[0], wa_ref[...])

    c = c_ref[0]
    cd = c - jnp.mean(c, axis=-1, keepdims=True)
    cn = cd * lax.rsqrt(jnp.mean(cd * cd, axis=-1, keepdims=True) + EPS_LN) * clng_ref[...] + clnb_ref[...]
    yb = _dot(cn * jax.nn.sigmoid(cn), wb_ref[...])

    yc = _dot(f_ref[0], wc_ref[...])

    gate = lambda j: gt_ref[0, :, j * D_MODEL:(j + 1) * D_MODEL].astype(F32)
    merged = gate(0) * ya + gate(1) * yb + gate(2) * yc
    o_ref[0] = x_ref[0] + _dot(merged, wo_ref[...])


def _merge(x, yf, yb, bon, g, c, f, gates, lw, tm=256):
    b, s, _ = x.shape
    full = lambda a: pl.BlockSpec(a.shape, lambda bi, i: (0,) * a.ndim)
    tok = lambda n: pl.BlockSpec((1, tm, n), lambda bi, i: (bi, i, 0))
    consts = (lw["lnx_g"], lw["lnx_b"], lw["conv_ln_g"], lw["conv_ln_b"], lw["bd_mean"],
              lw["w_a_out"], lw["w_b_out"], lw["w_c_out"], lw["w_out"])
    return pl.pallas_call(
        _merge_kernel,
        grid=(b, _tiles(s, tm)),
        in_specs=[tok(D_MODEL), tok(D_A), tok(D_A), tok(D_A), tok(D_A), tok(D_B), tok(D_C), tok(3 * D_MODEL)] + [full(a) for a in consts],
        out_specs=tok(D_MODEL),
        out_shape=jax.ShapeDtypeStruct(x.shape, F32),
        compiler_params=_params("parallel", "parallel"),
        name="merge",
    )(x, yf, yb, bon, g, c, f, gates, *consts)


def _norm_proj_kernel(x_ref, g_ref, w_ref, o_ref):
    o_ref[...] = _dot(_rms(x_ref[...], g_ref[...]), w_ref[...]).astype(o_ref.dtype)


def _norm_proj(x2, g, w, out_dtype, tm=256):
    t = x2.shape[0]
    n = w.shape[1]
    return pl.pallas_call(
        _norm_proj_kernel,
        grid=(_tiles(t, tm),),
        in_specs=[pl.BlockSpec((tm, D_MODEL), lambda i: (i, 0)),
                  pl.BlockSpec((1, D_MODEL), lambda i: (0, 0)),
                  pl.BlockSpec(w.shape, lambda i: (0, 0))],
        out_specs=pl.BlockSpec((tm, n), lambda i: (i, 0)),
        out_shape=jax.ShapeDtypeStruct((t, n), out_dtype),
        compiler_params=_params("parallel"),
        name="norm_proj",
    )(x2, g, w)


def _xattn_kernel(x_ref, kv_ref, g_ref, wq_ref, wo_ref, o_ref):
    x = x_ref[0]
    q = (_dot(_rms(x, g_ref[...]), wq_ref[...]) * HD_X ** -0.5).astype(BF16)
    heads = []
    for h in range(H_X):
        sl = slice(h * HD_X, (h + 1) * HD_X)
        sc = _dot_nt(q[:, sl], kv_ref[0, :, sl])
        e = jnp.exp(sc - jnp.max(sc, axis=-1, keepdims=True))
        p = e / jnp.sum(e, axis=-1, keepdims=True)
        heads.append(_dot(p, kv_ref[0, :, D_MODEL + h * HD_X:D_MODEL + (h + 1) * HD_X]))
    o_ref[0] = x + _dot(jnp.concatenate(heads, axis=1), wo_ref[...])


def _xattn(x, kv, lw, tm=512):
    b, s, _ = x.shape
    full = lambda a: pl.BlockSpec(a.shape, lambda bi, i: (0,) * a.ndim)
    return pl.pallas_call(
        _xattn_kernel,
        grid=(b, _tiles(s, tm)),
        in_specs=[pl.BlockSpec((1, tm, D_MODEL), lambda bi, i: (bi, i, 0)),
                  pl.BlockSpec((1, N_MEM, 2 * D_MODEL), lambda bi, i: (bi, 0, 0)),
                  full(lw["norm_x_g"]), full(lw["wq"]), full(lw["wo"])],
        out_specs=pl.BlockSpec((1, tm, D_MODEL), lambda bi, i: (bi, i, 0)),
        out_shape=jax.ShapeDtypeStruct(x.shape, F32),
        compiler_params=_params("parallel", "parallel"),
        name="cross_attn",
    )(x, kv, lw["norm_x_g"], lw["wq"], lw["wo"])


def _mlp_kernel(x_ref, g_ref, w1_ref, w2_ref, fg_ref, o_ref, hn_ref, acc_ref, *, final):
    j = pl.program_id(1)

    @pl.when(j == 0)
    def _():
        hn_ref[...] = _rms(x_ref[...], g_ref[...]).astype(BF16)
        acc_ref[...] = jnp.zeros_like(acc_ref)

    h1 = jnp.maximum(jnp.dot(hn_ref[...], w1_ref[...], preferred_element_type=F32), 0.0)
    acc_ref[...] += _dot(h1 * h1, w2_ref[...])

    @pl.when(j == pl.num_programs(1) - 1)
    def _():
        y = x_ref[...] + acc_ref[...]
        o_ref[...] = _rms(y, fg_ref[...]) if final else y


def _mlp(x2, lw, final_g, final, tm=1024, tf=1024):
    t = x2.shape[0]
    return pl.pallas_call(
        functools.partial(_mlp_kernel, final=final),
        grid=(_tiles(t, tm), _tiles(D_FF, tf)),
        in_specs=[pl.BlockSpec((tm, D_MODEL), lambda i, j: (i, 0)),
                  pl.BlockSpec((1, D_MODEL), lambda i, j: (0, 0)),
                  pl.BlockSpec((D_MODEL, tf), lambda i, j: (0, j)),
                  pl.BlockSpec((tf, D_MODEL), lambda i, j: (j, 0)),
                  pl.BlockSpec((1, D_MODEL), lambda i, j: (0, 0))],
        out_specs=pl.BlockSpec((tm, D_MODEL), lambda i, j: (i, 0)),
        out_shape=jax.ShapeDtypeStruct((t, D_MODEL), F32),
        scratch_shapes=[pltpu.VMEM((tm, D_MODEL), BF16), pltpu.VMEM((tm, D_MODEL), F32)],
        compiler_params=_params("parallel", "arbitrary"),
        name="mlp",
    )(x2, lw["norm_mlp_g"], lw["w_mlp1"], lw["w_mlp2"], final_g)


def _stage_layer(p, l):
    row = lambda a: a.reshape(1, -1).astype(F32)
    pad_lora = lambda w: jnp.stack([
        jnp.zeros((2 * LORA, D_A), F32).at[d * LORA:(d + 1) * LORA].set(w[d]) for d in range(2)]).astype(BF16)
    hid = jnp.arange(D_A, dtype=jnp.int32) // HEAD_A
    same_head = (hid[:, None] == hid[None, :]).astype(F32)
    return dict(
        norm_mix_g=row(p["norm_mix_g"][l]), w_in=p["w_in"][l].astype(BF16),
        mu=p["shift_mu"][l], w0=p["w0"][l], wup=pad_lora(p["w_up"][l]), a0=p["a0"][l],
        aup=pad_lora(p["a_up"][l]), gup=p["g_up"][l].astype(BF16),
        k_k=row(p["k_k"][l]), k_a=row(p["k_a"][l]), r_k=row(p["r_k"][l]),
        lnx_g=row(p["lnx_g"][l]), lnx_b=row(p["lnx_b"][l]),
        bd_ones=same_head.astype(BF16), bd_mean=(same_head / HEAD_A).astype(BF16),
        conv_w=jnp.concatenate([p["conv_w"][l], jnp.zeros((1, D_B), F32)], axis=0), conv_b=row(p["conv_b"][l]),
        conv_ln_g=row(p["conv_ln_g"][l]), conv_ln_b=row(p["conv_ln_b"][l]),
        gate_b=row(p["gate_b"][l]),
        w_a_out=p["w_a_out"][l].astype(BF16), w_b_out=p["w_b_out"][l].astype(BF16),
        w_c_out=p["w_c_out"][l].astype(BF16), w_out=p["w_out"][l].astype(BF16),
        norm_x_g=row(p["norm_x_g"][l]), norm_mem_g=row(p["norm_mem_g"][l]),
        wq=p["wq"][l].astype(BF16), wo=p["wo"][l].astype(BF16),
        wkv=jnp.concatenate([p["wk"][l], p["wv"][l]], axis=1).astype(BF16),
        norm_mlp_g=row(p["norm_mlp_g"][l]),
        w_mlp1=p["w_mlp1"][l].astype(BF16), w_mlp2=p["w_mlp2"][l].astype(BF16),
    )


def _trunk(x, mem, layers, final_g):
    b, s, _ = x.shape
    t = b * s
    cw, rev, dm = _dft_tables(s)
    mem2 = mem.reshape(b * N_MEM, D_MODEL)
    for l, lw in enumerate(layers):
        u, zc, gates, r, v, na, g, bon, lwd, kd, bb = _in_proj(x, lw)
        yf, yb = _wkv(r, v, na, lwd, kd, bb)
        c = _conv(u, lw)
        f = _dft(zc, cw, rev, dm)
        x = _merge(x, yf, yb, bon, g, c, f, gates, lw)
        kv = _norm_proj(mem2, lw["norm_mem_g"], lw["wkv"], BF16).reshape(b, N_MEM, 2 * D_MODEL)
        x = _xattn(x, kv, lw)
        x = _mlp(x.reshape(t, D_MODEL), lw, final_g, l == len(layers) - 1).reshape(b, s, D_MODEL)
    return x


def kernel(x_prompt, x_sample, mem_prompt, mem_sample, norm_mix_g, w_in, shift_mu, w0, w_up, a0, a_up, g_up, k_k, k_a, r_k, lnx_g, lnx_b, w_a_out, conv_w, conv_b, conv_ln_g, conv_ln_b, w_b_out, w_c_out, gate_b, w_out, norm_x_g, norm_mem_g, wq, wk, wv, wo, norm_mlp_g, w_mlp1, w_mlp2, final_norm_g):
    p = dict(norm_mix_g=norm_mix_g, w_in=w_in, shift_mu=shift_mu, w0=w0, w_up=w_up, a0=a0,
             a_up=a_up, g_up=g_up, k_k=k_k, k_a=k_a, r_k=r_k, lnx_g=lnx_g, lnx_b=lnx_b,
             w_a_out=w_a_out, conv_w=conv_w, conv_b=conv_b, conv_ln_g=conv_ln_g,
             conv_ln_b=conv_ln_b, w_b_out=w_b_out, w_c_out=w_c_out, gate_b=gate_b, w_out=w_out,
             norm_x_g=norm_x_g, norm_mem_g=norm_mem_g, wq=wq, wk=wk, wv=wv, wo=wo,
             norm_mlp_g=norm_mlp_g, w_mlp1=w_mlp1, w_mlp2=w_mlp2)
    layers = [_stage_layer(p, l) for l in range(DEPTH)]
    final_g = final_norm_g.reshape(1, D_MODEL).astype(F32)
    return (_trunk(x_prompt, mem_prompt, layers, final_g), _trunk(x_sample, mem_sample, layers, final_g))
```

```python
import functools
import math

import jax
import jax.numpy as jnp
from jax import lax
from jax.experimental import pallas as pl
from jax.experimental.pallas import tpu as pltpu

F32 = jnp.float32
BF16 = jnp.bfloat16

D_MODEL = 1024
DEPTH = 4
N_MEM = 256
D_A = 512
HEAD_A = 64
LORA = 64
LORA_G = 128
N_A_IN = 3 * D_A + 4 * LORA + LORA_G
D_B = 512
CONV_K = 31
D_C = 512
GW_C = 128
H_X = 4
HD_X = D_MODEL // H_X
D_FF = 4 * D_MODEL

EPS_RMS = 1e-6
EPS_LN = 1e-5
EPS_GN = 64e-5

VMEM_LIMIT_BYTES = 56 * 1024 * 1024

WKV_CHUNK = 64
WKV_HALF = 128
CONV_HALO = 16
IN_HALO = 16
GATE_CHUNK = 512
SUBLANES = 8
DFT_BLK = 256
NYQ_ROWS = 16


def _params(*sem):
    return pltpu.CompilerParams(dimension_semantics=sem, vmem_limit_bytes=VMEM_LIMIT_BYTES)


def _tiles(n, tile):
    assert n % tile == 0, (n, tile)
    return n // tile


def _dot(a, b):
    return jnp.dot(a.astype(BF16), b.astype(BF16), preferred_element_type=F32)


def _dot_nt(a, b):
    return lax.dot_general(a.astype(BF16), b.astype(BF16), (((1,), (1,)), ((), ())),
                           preferred_element_type=F32)


def _split_dot(x, m):
    xh = x.astype(BF16)
    xl = (x - xh.astype(F32)).astype(BF16)
    return jnp.dot(xh, m, preferred_element_type=F32) + jnp.dot(xl, m, preferred_element_type=F32)


def _split_dot_lhs(m, x):
    xh = x.astype(BF16)
    xl = (x - xh.astype(F32)).astype(BF16)
    return jnp.dot(m, xh, preferred_element_type=F32) + jnp.dot(m, xl, preferred_element_type=F32)


def _rms(x, g):
    return x * lax.rsqrt(jnp.mean(x * x, axis=-1, keepdims=True) + EPS_RMS) * g


def _in_proj_kernel(x_ref, xp_ref, xn_ref, g_ref, w_ref, gb_ref, mu_ref, w0_ref, wup_ref, a0_ref, aup_ref,
                    gup_ref, kk_ref, ka_ref, rk_ref, bd_ref,
                    u_ref, zc_ref, gt_ref, r_ref, v_ref, na_ref, g_out_ref, bon_ref, lw_ref, kd_ref, b_ref,
                    za_ref):
    i = pl.program_id(1)
    n = pl.num_programs(1)
    tm = x_ref.shape[1]
    hn = _rms(x_ref[0], g_ref[...]).astype(BF16)
    hp = jnp.where(i > 0, _rms(xp_ref[0], g_ref[...]), 0.0).astype(BF16)
    hx = jnp.where(i < n - 1, _rms(xn_ref[0], g_ref[...]), 0.0).astype(BF16)
    proj = lambda h, lo, width: jnp.dot(h, w_ref[:, lo:lo + width], preferred_element_type=F32)

    gate_lo = N_A_IN + 2 * D_B + D_C

    def gate_chunk(j):
        cols = slice(j * GATE_CHUNK, (j + 1) * GATE_CHUNK)
        gt_ref[0, :, cols] = jax.nn.sigmoid(
            proj(hn, gate_lo + j * GATE_CHUNK, GATE_CHUNK) + gb_ref[:, cols]).astype(BF16)

    zb = proj(hn, N_A_IN, 2 * D_B)
    za_ref[...] = proj(jnp.concatenate([hp, hn, hx], axis=0), 0, N_A_IN)
    u_ref[0] = zb[:, :D_B] * jax.nn.sigmoid(zb[:, D_B:])

    gate_chunk(0)
    z = za_ref[IN_HALO:IN_HALO + tm, :]
    zprev = za_ref[IN_HALO - 1:IN_HALO - 1 + tm, :]
    znext = za_ref[IN_HALO + 1:IN_HALO + 1 + tm, :]
    zs = z + mu_ref[0:1, :] * (zprev - z) + mu_ref[1:2, :] * (znext - z)
    r = zs[:, 0:D_A]
    k = zs[:, D_A:2 * D_A]
    v = zs[:, 2 * D_A:3 * D_A]
    o = 3 * D_A
    tdw = jnp.tanh(zs[:, o:o + 2 * LORA]).astype(BF16)
    dab = zs[:, o + 2 * LORA:o + 4 * LORA].astype(BF16)
    dg = zs[:, o + 4 * LORA:o + 4 * LORA + LORA_G]
    bd = bd_ref[...]
    r_ref[0] = r
    v_ref[0] = v

    gate_chunk(1)
    g_out_ref[0] = _dot(jax.nn.sigmoid(dg), gup_ref[...])
    kk0 = k * kk_ref[...]
    nrm = jnp.sqrt(_dot(kk0 * kk0, bd))
    kk = kk0 / jnp.maximum(nrm, 1e-12)
    na_ref[0] = -kk
    ksum = jnp.zeros_like(k)
    for d in range(2):
        gate_chunk(2 + 2 * d)
        wl = w0_ref[d:d + 1, :] + jnp.dot(tdw, wup_ref[d], preferred_element_type=F32)
        lw_ref[0, d] = (-math.exp(-0.5)) * jax.nn.sigmoid(wl)
        a = jax.nn.sigmoid(a0_ref[d:d + 1, :] + jnp.dot(dab, aup_ref[d], preferred_element_type=F32))
        gate_chunk(3 + 2 * d)
        kd = k * (1.0 + (a - 1.0) * ka_ref[...])
        kd_ref[0, d] = kd
        b_ref[0, d] = kk * a
        ksum = ksum + kd
    zc_ref[0] = proj(hn, N_A_IN + 2 * D_B, D_C).astype(BF16)
    bon_ref[0] = _dot(r * rk_ref[...] * ksum, bd) * v


def _in_proj(x, lw, tm=256):
    b, s, _ = x.shape
    nh = tm // IN_HALO
    last = s // IN_HALO - 1
    const = lambda a: pl.BlockSpec(a.shape, lambda bi, i: (0,) * a.ndim, pipeline_mode=pl.Buffered(1))
    consts = (lw["norm_mix_g"], lw["w_in"], lw["gate_b"], lw["mu"], lw["w0"], lw["wup"], lw["a0"], lw["aup"],
              lw["gup"], lw["k_k"], lw["k_a"], lw["r_k"], lw["bd_ones"])
    tok = lambda width: pl.BlockSpec((1, tm, width), lambda bi, i: (bi, i, 0))
    dirs = pl.BlockSpec((1, 2, tm, D_A), lambda bi, i: (bi, 0, i, 0))
    tok_shape = lambda width, dt: jax.ShapeDtypeStruct((b, s, width), dt)
    dir_shape = jax.ShapeDtypeStruct((b, 2, s, D_A), F32)
    return pl.pallas_call(
        _in_proj_kernel,
        grid=(b, _tiles(s, tm)),
        in_specs=[tok(D_MODEL),
                  pl.BlockSpec((1, IN_HALO, D_MODEL), lambda bi, i: (bi, jnp.maximum(i * nh - 1, 0), 0)),
                  pl.BlockSpec((1, IN_HALO, D_MODEL), lambda bi, i: (bi, jnp.minimum((i + 1) * nh, last), 0)),
                  ] + [const(c) for c in consts],
        out_specs=[tok(D_B), tok(D_C), tok(3 * D_MODEL)] + [tok(D_A)] * 5 + [dirs] * 3,
        out_shape=[tok_shape(D_B, F32), tok_shape(D_C, BF16), tok_shape(3 * D_MODEL, BF16)]
        + [tok_shape(D_A, F32)] * 5 + [dir_shape] * 3,
        scratch_shapes=[pltpu.VMEM((tm + 2 * IN_HALO, N_A_IN), F32)],
        compiler_params=_params("parallel", "parallel"),
        name="in_proj",
    )(x, x, x, *consts)


def _wkv_masks(reverse):
    C, HW = WKV_CHUNK, WKV_HALF
    row = lax.broadcasted_iota(jnp.int32, (C, HW), 0)
    lane_i = lax.broadcasted_iota(jnp.int32, (C, HW), 1) & (C - 1)
    dt = (lane_i - row) if reverse else (row - lane_i)
    ti = lax.broadcasted_iota(jnp.int32, (C, C), 0)
    tj = lax.broadcasted_iota(jnp.int32, (C, C), 1)
    rr = lax.broadcasted_iota(jnp.int32, (HW, HW), 0)
    cc = lax.broadcasted_iota(jnp.int32, (HW, HW), 1)
    return dict(tri=jnp.where((tj >= ti) if reverse else (tj <= ti), 1.0, 0.0).astype(BF16),
                strict=dt > 0, incl=dt >= 0, eye=dt == 0,
                bd=(rr >> 6) == (cc >> 6), diag=rr == cc)


def _wkv_chunks(chains):
    C, HW = WKV_CHUNK, WKV_HALF
    n = range(len(chains))
    r, a, b, k, v, lw, sbd, m, rev = zip(*chains)
    bd = lambda i, x: jnp.where(m[i]["bd"], jnp.concatenate([x] * (HW // C), axis=0), 0.0)

    cum = [_split_dot_lhs(m[i]["tri"], lw[i]) for i in n]
    tot = [cum[i][0:1, :] if rev[i] else cum[i][C - 1:C, :] for i in n]
    e_inv = [jnp.exp(-cum[i]) for i in n]
    e_rem = [jnp.exp(tot[i] - cum[i]) for i in n]
    at = [a[i] * jnp.exp(cum[i] - lw[i]) for i in n]
    rt = [r[i] * jnp.exp(cum[i]) for i in n]

    gram = [_dot_nt(jnp.concatenate([at[i], rt[i]], axis=0),
                    jnp.concatenate([bd(i, b[i] * e_inv[i]), bd(i, k[i] * e_inv[i])], axis=0)) for i in n]
    l_ab = [jnp.where(m[i]["strict"], gram[i][:C, :HW], 0.0) for i in n]
    l_ak = [jnp.where(m[i]["strict"], gram[i][:C, HW:], 0.0) for i in n]
    m_rb = [jnp.where(m[i]["incl"], gram[i][C:, :HW], 0.0) for i in n]
    m_rk = [jnp.where(m[i]["incl"], gram[i][C:, HW:], 0.0) for i in n]

    t_inv = [jnp.where(m[i]["eye"], 1.0, 0.0) + l_ab[i] for i in n]
    lp = [_dot(l_ab[i], bd(i, l_ab[i])) for i in n]
    nsq = int(math.log2(C)) - 1
    for j in range(1, nsq):
        res = [_dot(jnp.concatenate([lp[i], t_inv[i]], axis=0), bd(i, lp[i])) for i in n]
        lp = [res[i][:C] for i in n]
        t_inv = [t_inv[i] + res[i][C:] for i in n]
    t_inv = [t_inv[i] + _dot(t_inv[i], bd(i, lp[i])) for i in n]

    lv = [_dot(jnp.concatenate([l_ak[i], m_rk[i]], axis=0), bd(i, v[i])) for i in n]
    pq = [_dot(t_inv[i], jnp.concatenate([bd(i, at[i]), bd(i, lv[i][:C])], axis=1)) for i in n]
    mpq = [_dot(m_rb[i], jnp.concatenate([bd(i, pq[i][:, :HW]), bd(i, pq[i][:, HW:])], axis=1)) for i in n]
    rp = [rt[i] + mpq[i][:, :HW] for i in n]
    yq = [mpq[i][:, HW:] + lv[i][C:] for i in n]

    o1 = [_dot(jnp.concatenate([pq[i][:, :HW], rp[i]], axis=0), sbd[i]) for i in n]
    lhs_t = [jnp.transpose(jnp.concatenate([b[i] * e_rem[i], k[i] * e_rem[i]], axis=0)) for i in n]
    upd = [_dot(lhs_t[i], jnp.concatenate([o1[i][:C] + pq[i][:, HW:], v[i]], axis=0)) for i in n]
    wc_col = [jnp.sum(jnp.where(m[i]["diag"], jnp.exp(tot[i]), 0.0), axis=1, keepdims=True) for i in n]
    return [(o1[i][C:] + yq[i], wc_col[i] * sbd[i] + jnp.where(m[i]["bd"], upd[i], 0.0)) for i in n]


def _wkv_kernel(rf_ref, vf_ref, af_ref, lwf_ref, kdf_ref, bf_ref,
                rb_ref, vb_ref, ab_ref, lwb_ref, kdb_ref, bb_ref, yf_ref, yb_ref, s_ref, *, nb):
    @pl.when(pl.program_id(1) == 0)
    def _():
        s_ref[...] = jnp.zeros_like(s_ref)

    dirs = ((rf_ref, vf_ref, af_ref, lwf_ref, kdf_ref, bf_ref, yf_ref),
            (rb_ref, vb_ref, ab_ref, lwb_ref, kdb_ref, bb_ref, yb_ref))
    masks = (_wkv_masks(False), _wkv_masks(True))
    chains, dests = [], []
    for bi in range(nb):
        for d, (r_ref, v_ref, a_ref, lw_ref, kd_ref, b_ref, y_ref) in enumerate(dirs):
            for h in range(D_A // WKV_HALF):
                sl = slice(h * WKV_HALF, (h + 1) * WKV_HALF)
                chains.append((r_ref[bi, :, sl], a_ref[bi, :, sl], b_ref[bi, 0, :, sl], kd_ref[bi, 0, :, sl],
                               v_ref[bi, :, sl], lw_ref[bi, 0, :, sl], s_ref[bi, d, h], masks[d], d == 1))
                dests.append((y_ref, bi, sl, d, h))
    for (y, s_new), (y_ref, bi, sl, d, h) in zip(_wkv_chunks(chains), dests):
        y_ref[bi, :, sl] = y
        s_ref[bi, d, h] = s_new


def _wkv(r, v, na, lw, kd, bb, nb=2):
    b, s, _ = r.shape
    nc = _tiles(s, WKV_CHUNK)
    tok_f = pl.BlockSpec((nb, WKV_CHUNK, D_A), lambda bi, c: (bi, c, 0))
    tok_b = pl.BlockSpec((nb, WKV_CHUNK, D_A), lambda bi, c: (bi, nc - 1 - c, 0))
    dir_f = pl.BlockSpec((nb, 1, WKV_CHUNK, D_A), lambda bi, c: (bi, 0, c, 0))
    dir_b = pl.BlockSpec((nb, 1, WKV_CHUNK, D_A), lambda bi, c: (bi, 1, nc - 1 - c, 0))
    y_shape = jax.ShapeDtypeStruct((b, s, D_A), F32)
    return pl.pallas_call(
        functools.partial(_wkv_kernel, nb=nb),
        grid=(_tiles(b, nb), nc),
        in_specs=[tok_f] * 3 + [dir_f] * 3 + [tok_b] * 3 + [dir_b] * 3,
        out_specs=[tok_f, tok_b],
        out_shape=[y_shape, y_shape],
        scratch_shapes=[pltpu.VMEM((nb, 2, D_A // WKV_HALF, WKV_HALF, WKV_HALF), F32)],
        compiler_params=_params("parallel", "arbitrary"),
        name="wkv_scan",
    )(r, v, na, lw, kd, bb, r, v, na, lw, kd, bb)


def _dft_kernel(x_ref, cw_ref, rev_ref, dm_ref, o_ref, rhs_ref, nyq_ref):
    i = pl.program_id(1)
    s = x_ref.shape[1]
    tm = o_ref.shape[1]
    half = s // 2
    nblk = s // DFT_BLK
    groups = [slice(g * GW_C, (g + 1) * GW_C) for g in range(D_C // GW_C)]

    @pl.when(i == 0)
    def _():
        for blk in range(half // DFT_BLK):
            rows = slice(blk * DFT_BLK, (blk + 1) * DFT_BLK)
            hi = x_ref[0, (nblk - blk - 1) * DFT_BLK:(nblk - blk) * DFT_BLK, :]
            if blk == 0:
                xr = jnp.dot(rev_ref[:, :DFT_BLK], hi, preferred_element_type=F32)
            else:
                top = x_ref[0, (nblk - blk) * DFT_BLK:(nblk - blk + 1) * DFT_BLK, :]
                xr = jnp.dot(rev_ref[...], jnp.concatenate([hi, top], axis=0), preferred_element_type=F32)
            xa = x_ref[0, rows, :].astype(F32)
            xe = (xa + xr).astype(BF16)
            xo = (xa - xr).astype(BF16)
            for sl in groups:
                rhs_ref[rows, sl] = jnp.dot(xe[:, sl], cw_ref[:, :GW_C], preferred_element_type=F32).astype(BF16)
                rhs_ref[half + blk * DFT_BLK:half + (blk + 1) * DFT_BLK, sl] = jnp.dot(
                    xo[:, sl], cw_ref[:, GW_C:], preferred_element_type=F32).astype(BF16)
        mid = x_ref[0, half:half + NYQ_ROWS, :]
        for sl in groups:
            nyq_ref[:, sl] = jnp.dot(mid[:, sl], cw_ref[:, :GW_C], preferred_element_type=F32) * s ** -0.5

    odd = (lax.broadcasted_iota(jnp.int32, (tm, D_C), 0) & 1) == 1
    nyq = nyq_ref[0:1, :]
    o_ref[0] = (jnp.dot(dm_ref[...], rhs_ref[...], preferred_element_type=F32)
                + jnp.where(odd, -nyq, nyq)).astype(o_ref.dtype)


def _dft(zc, cw, rev, dm, tm=512):
    b, s, _ = zc.shape
    assert tm % 2 == 0 and (s // 2) % DFT_BLK == 0
    return pl.pallas_call(
        _dft_kernel,
        grid=(b, _tiles(s, tm)),
        in_specs=[pl.BlockSpec((1, s, D_C), lambda bi, i: (bi, 0, 0)),
                  pl.BlockSpec(cw.shape, lambda bi, i: (0, 0)),
                  pl.BlockSpec(rev.shape, lambda bi, i: (0, 0)),
                  pl.BlockSpec((tm, s), lambda bi, i: (i, 0))],
        out_specs=pl.BlockSpec((1, tm, D_C), lambda bi, i: (bi, i, 0)),
        out_shape=jax.ShapeDtypeStruct((b, s, D_C), BF16),
        scratch_shapes=[pltpu.VMEM((s, D_C), BF16), pltpu.VMEM((NYQ_ROWS, D_C), F32)],
        compiler_params=_params("parallel", "arbitrary"),
        name="dft2_real",
    )(zc, cw, rev, dm)


def _dft_tables(s):
    def cs(n, cols):
        ang = ((jnp.arange(n, dtype=jnp.int32)[:, None] * jnp.arange(cols, dtype=jnp.int32)[None, :]) % n
               ).astype(F32) * (2.0 * math.pi / n)
        return jnp.cos(ang), jnp.sin(ang)
    cc, sc = cs(GW_C, GW_C)
    cw = (jnp.concatenate([cc, sc], axis=1) * GW_C ** -0.5).astype(BF16)
    j = jnp.arange(DFT_BLK, dtype=jnp.int32)[:, None]
    c = jnp.arange(2 * DFT_BLK, dtype=jnp.int32)[None, :]
    rev = (((j >= 1) & (c == DFT_BLK - j)) | ((j == 0) & (c == DFT_BLK))).astype(BF16)
    cp, sp = cs(s, s // 2)
    dm = (jnp.concatenate([cp, -sp], axis=1) * s ** -0.5).astype(BF16)
    return cw, rev, dm


def _merge_kernel(x_ref, yf_ref, yb_ref, bon_ref, g_ref, u_ref, up_ref, un_ref, f_ref, gt_ref,
                  lnxg_ref, lnxb_ref, cw_ref, cb_ref, clng_ref, clnb_ref, bdm_ref,
                  wa_ref, wb_ref, wc_ref, wo_ref, o_ref, us_ref, c_ref, *, sub):
    i = pl.program_id(1)
    n = pl.num_programs(1)
    tm = x_ref.shape[1]
    rows = tm + 2 * CONV_HALO

    bdm = bdm_ref[...]
    ys = yf_ref[0] + yb_ref[0]
    dev = ys - _split_dot(ys, bdm)
    var = _dot(dev * dev, bdm)
    yn = dev * lax.rsqrt(var + EPS_GN) * lnxg_ref[...] + lnxb_ref[...]
    ya = _dot((yn + bon_ref[0]) * g_ref[0], wa_ref[...])
    yc = _dot(f_ref[0], wc_ref[...])

    us_ref[0, 0:CONV_HALO, :] = jnp.where(i > 0, up_ref[0], 0.0)
    us_ref[0, CONV_HALO:CONV_HALO + tm, :] = u_ref[0]
    us_ref[0, CONV_HALO + tm:rows, :] = jnp.where(i < n - 1, un_ref[0], 0.0)
    for s in range(1, SUBLANES):
        us_ref[s, 0:rows - SUBLANES, :] = us_ref[0, s:s + rows - SUBLANES, :]
    first = CONV_HALO - CONV_K // 2
    for t0 in range(0, tm, sub):
        acc = jnp.zeros((sub, D_B), F32) + cb_ref[...]
        for j in range(CONV_K):
            q, s = divmod(first + j, SUBLANES)
            acc = acc + cw_ref[j:j + 1, :] * us_ref[s, t0 + q * SUBLANES:t0 + q * SUBLANES + sub, :]
        c_ref[t0:t0 + sub, :] = acc
    c = c_ref[...]
    cd = c - jnp.mean(c, axis=-1, keepdims=True)
    cn = cd * lax.rsqrt(jnp.mean(cd * cd, axis=-1, keepdims=True) + EPS_LN) * clng_ref[...] + clnb_ref[...]
    yb = _dot(cn * jax.nn.sigmoid(cn), wb_ref[...])

    gate = lambda j: gt_ref[0, :, j * D_MODEL:(j + 1) * D_MODEL].astype(F32)
    merged = gate(0) * ya + gate(1) * yb + gate(2) * yc
    o_ref[0] = x_ref[0] + _dot(merged, wo_ref[...])


def _merge(x, yf, yb, bon, g, u, f, gates, lw, tm=256, sub=32):
    b, s, _ = x.shape
    nh = tm // CONV_HALO
    last = s // CONV_HALO - 1
    const = lambda a: pl.BlockSpec(a.shape, lambda bi, i: (0,) * a.ndim, pipeline_mode=pl.Buffered(1))
    tok = lambda n: pl.BlockSpec((1, tm, n), lambda bi, i: (bi, i, 0))
    consts = (lw["lnx_g"], lw["lnx_b"], lw["conv_w"], lw["conv_b"], lw["conv_ln_g"], lw["conv_ln_b"],
              lw["bd_mean"], lw["w_a_out"], lw["w_b_out"], lw["w_c_out"], lw["w_out"])
    return pl.pallas_call(
        functools.partial(_merge_kernel, sub=sub),
        grid=(b, _tiles(s, tm)),
        in_specs=[tok(D_MODEL), tok(D_A), tok(D_A), tok(D_A), tok(D_A), tok(D_B),
                  pl.BlockSpec((1, CONV_HALO, D_B), lambda bi, i: (bi, jnp.maximum(i * nh - 1, 0), 0)),
                  pl.BlockSpec((1, CONV_HALO, D_B), lambda bi, i: (bi, jnp.minimum((i + 1) * nh, last), 0)),
                  tok(D_C), tok(3 * D_MODEL)] + [const(a) for a in consts],
        out_specs=tok(D_MODEL),
        out_shape=jax.ShapeDtypeStruct(x.shape, F32),
        scratch_shapes=[pltpu.VMEM((SUBLANES, tm + 2 * CONV_HALO, D_B), F32), pltpu.VMEM((tm, D_B), F32)],
        compiler_params=_params("parallel", "parallel"),
        name="merge",
    )(x, yf, yb, bon, g, u, u, u, f, gates, *consts)


def _norm_proj_kernel(x_ref, g_ref, w_ref, o_ref):
    o_ref[...] = _dot(_rms(x_ref[...], g_ref[...]), w_ref[...]).astype(o_ref.dtype)


def _norm_proj(x2, g, w, out_dtype, tm=256):
    t = x2.shape[0]
    n = w.shape[1]
    return pl.pallas_call(
        _norm_proj_kernel,
        grid=(_tiles(t, tm),),
        in_specs=[pl.BlockSpec((tm, D_MODEL), lambda i: (i, 0)),
                  pl.BlockSpec((1, D_MODEL), lambda i: (0, 0)),
                  pl.BlockSpec(w.shape, lambda i: (0, 0))],
        out_specs=pl.BlockSpec((tm, n), lambda i: (i, 0)),
        out_shape=jax.ShapeDtypeStruct((t, n), out_dtype),
        compiler_params=_params("parallel"),
        name="norm_proj",
    )(x2, g, w)


def _xattn_kernel(x_ref, kv_ref, g_ref, wq_ref, wo_ref, o_ref):
    x = x_ref[0]
    q = (_dot(_rms(x, g_ref[...]), wq_ref[...]) * HD_X ** -0.5).astype(BF16)
    heads = []
    for h in range(H_X):
        sl = slice(h * HD_X, (h + 1) * HD_X)
        sc = _dot_nt(q[:, sl], kv_ref[0, :, sl])
        e = jnp.exp(sc - jnp.max(sc, axis=-1, keepdims=True))
        p = e / jnp.sum(e, axis=-1, keepdims=True)
        heads.append(_dot(p, kv_ref[0, :, D_MODEL + h * HD_X:D_MODEL + (h + 1) * HD_X]))
    o_ref[0] = x + _dot(jnp.concatenate(heads, axis=1), wo_ref[...])


def _xattn(x, kv, lw, tm=1024):
    b, s, _ = x.shape
    full = lambda a: pl.BlockSpec(a.shape, lambda bi, i: (0,) * a.ndim)
    return pl.pallas_call(
        _xattn_kernel,
        grid=(b, _tiles(s, tm)),
        in_specs=[pl.BlockSpec((1, tm, D_MODEL), lambda bi, i: (bi, i, 0)),
                  pl.BlockSpec((1, N_MEM, 2 * D_MODEL), lambda bi, i: (bi, 0, 0)),
                  full(lw["norm_x_g"]), full(lw["wq"]), full(lw["wo"])],
        out_specs=pl.BlockSpec((1, tm, D_MODEL), lambda bi, i: (bi, i, 0)),
        out_shape=jax.ShapeDtypeStruct(x.shape, F32),
        compiler_params=_params("parallel", "parallel"),
        name="cross_attn",
    )(x, kv, lw["norm_x_g"], lw["wq"], lw["wo"])


def _mlp_kernel(x_ref, g_ref, w1_ref, w2_ref, fg_ref, o_ref, hn_ref, acc_ref, *, final):
    j = pl.program_id(1)

    @pl.when(j == 0)
    def _():
        hn_ref[...] = _rms(x_ref[...], g_ref[...]).astype(BF16)
        acc_ref[...] = jnp.zeros_like(acc_ref)

    h1 = jnp.maximum(jnp.dot(hn_ref[...], w1_ref[...], preferred_element_type=F32), 0.0)
    acc_ref[...] += _dot(h1 * h1, w2_ref[...])

    @pl.when(j == pl.num_programs(1) - 1)
    def _():
        y = x_ref[...] + acc_ref[...]
        o_ref[...] = _rms(y, fg_ref[...]) if final else y


def _mlp(x2, lw, final_g, final, tm=1024, tf=1024):
    t = x2.shape[0]
    return pl.pallas_call(
        functools.partial(_mlp_kernel, final=final),
        grid=(_tiles(t, tm), _tiles(D_FF, tf)),
        in_specs=[pl.BlockSpec((tm, D_MODEL), lambda i, j: (i, 0)),
                  pl.BlockSpec((1, D_MODEL), lambda i, j: (0, 0)),
                  pl.BlockSpec((D_MODEL, tf), lambda i, j: (0, j)),
                  pl.BlockSpec((tf, D_MODEL), lambda i, j: (j, 0)),
                  pl.BlockSpec((1, D_MODEL), lambda i, j: (0, 0))],
        out_specs=pl.BlockSpec((tm, D_MODEL), lambda i, j: (i, 0)),
        out_shape=jax.ShapeDtypeStruct((t, D_MODEL), F32),
        scratch_shapes=[pltpu.VMEM((tm, D_MODEL), BF16), pltpu.VMEM((tm, D_MODEL), F32)],
        compiler_params=_params("parallel", "arbitrary"),
        name="mlp",
    )(x2, lw["norm_mlp_g"], lw["w_mlp1"], lw["w_mlp2"], final_g)


def _stage_layer(p, l):
    row = lambda a: a.reshape(1, -1).astype(F32)
    pad_lora = lambda w: jnp.stack([
        jnp.zeros((2 * LORA, D_A), F32).at[d * LORA:(d + 1) * LORA].set(w[d]) for d in range(2)]).astype(BF16)
    hid = jnp.arange(D_A, dtype=jnp.int32) // HEAD_A
    same_head = (hid[:, None] == hid[None, :]).astype(F32)
    return dict(
        norm_mix_g=row(p["norm_mix_g"][l]), w_in=p["w_in"][l].astype(BF16),
        mu=p["shift_mu"][l], w0=p["w0"][l], wup=pad_lora(p["w_up"][l]), a0=p["a0"][l],
        aup=pad_lora(p["a_up"][l]), gup=p["g_up"][l].astype(BF16),
        k_k=row(p["k_k"][l]), k_a=row(p["k_a"][l]), r_k=row(p["r_k"][l]),
        lnx_g=row(p["lnx_g"][l]), lnx_b=row(p["lnx_b"][l]),
        bd_ones=same_head.astype(BF16), bd_mean=(same_head / HEAD_A).astype(BF16),
        conv_w=jnp.concatenate([p["conv_w"][l], jnp.zeros((1, D_B), F32)], axis=0), conv_b=row(p["conv_b"][l]),
        conv_ln_g=row(p["conv_ln_g"][l]), conv_ln_b=row(p["conv_ln_b"][l]),
        gate_b=row(p["gate_b"][l]),
        w_a_out=p["w_a_out"][l].astype(BF16), w_b_out=p["w_b_out"][l].astype(BF16),
        w_c_out=p["w_c_out"][l].astype(BF16), w_out=p["w_out"][l].astype(BF16),
        norm_x_g=row(p["norm_x_g"][l]), norm_mem_g=row(p["norm_mem_g"][l]),
        wq=p["wq"][l].astype(BF16), wo=p["wo"][l].astype(BF16),
        wkv=jnp.concatenate([p["wk"][l], p["wv"][l]], axis=1).astype(BF16),
        norm_mlp_g=row(p["norm_mlp_g"][l]),
        w_mlp1=p["w_mlp1"][l].astype(BF16), w_mlp2=p["w_mlp2"][l].astype(BF16),
    )


def _trunk(x, mem, layers, final_g):
    b, s, _ = x.shape
    t = b * s
    cw, rev, dm = _dft_tables(s)
    mem2 = mem.reshape(b * N_MEM, D_MODEL)
    for l, lw in enumerate(layers):
        u, zc, gates, r, v, na, g, bon, lwd, kd, bb = _in_proj(x, lw)
        yf, yb = _wkv(r, v, na, lwd, kd, bb)
        f = _dft(zc, cw, rev, dm)
        x = _merge(x, yf, yb, bon, g, u, f, gates, lw)
        kv = _norm_proj(mem2, lw["norm_mem_g"], lw["wkv"], BF16).reshape(b, N_MEM, 2 * D_MODEL)
        x = _xattn(x, kv, lw)
        x = _mlp(x.reshape(t, D_MODEL), lw, final_g, l == len(layers) - 1).reshape(b, s, D_MODEL)
    return x


def kernel(x_prompt, x_sample, mem_prompt, mem_sample, norm_mix_g, w_in, shift_mu, w0, w_up, a0, a_up, g_up, k_k, k_a, r_k, lnx_g, lnx_b, w_a_out, conv_w, conv_b, conv_ln_g, conv_ln_b, w_b_out, w_c_out, gate_b, w_out, norm_x_g, norm_mem_g, wq, wk, wv, wo, norm_mlp_g, w_mlp1, w_mlp2, final_norm_g):
    p = dict(norm_mix_g=norm_mix_g, w_in=w_in, shift_mu=shift_mu, w0=w0, w_up=w_up, a0=a0,
             a_up=a_up, g_up=g_up, k_k=k_k, k_a=k_a, r_k=r_k, lnx_g=lnx_g, lnx_b=lnx_b,
             w_a_out=w_a_out, conv_w=conv_w, conv_b=conv_b, conv_ln_g=conv_ln_g,
             conv_ln_b=conv_ln_b, w_b_out=w_b_out, w_c_out=w_c_out, gate_b=gate_b, w_out=w_out,
             norm_x_g=norm_x_g, norm_mem_g=norm_mem_g, wq=wq, wk=wk, wv=wv, wo=wo,
             norm_mlp_g=norm_mlp_g, w_mlp1=w_mlp1, w_mlp2=w_mlp2)
    layers = [_stage_layer(p, l) for l in range(DEPTH)]
    final_g = final_norm_g.reshape(1, D_MODEL).astype(F32)
    return (_trunk(x_prompt, mem_prompt, layers, final_g), _trunk(x_sample, mem_sample, layers, final_g))
```

```python
import functools
import math

import jax
import jax.numpy as jnp
from jax import lax
from jax.experimental import pallas as pl
from jax.experimental.pallas import tpu as pltpu

F32 = jnp.float32
BF16 = jnp.bfloat16

D_MODEL = 1024
DEPTH = 4
N_MEM = 256
D_A = 512
HEAD_A = 64
LORA = 64
LORA_G = 128
N_A_IN = 3 * D_A + 4 * LORA + LORA_G
D_B = 512
CONV_K = 31
D_C = 512
GW_C = 128
H_X = 4
HD_X = D_MODEL // H_X
D_FF = 4 * D_MODEL

EPS_RMS = 1e-6
EPS_LN = 1e-5
EPS_GN = 64e-5

VMEM_LIMIT_BYTES = 56 * 1024 * 1024

WKV_CHUNK = 64
WKV_HALF = 128
CONV_HALO = 16
IN_HALO = 16
GATE_CHUNK = 512
SUBLANES = 8
DFT_BLK = 256
NYQ_ROWS = 16


def _params(*sem):
    return pltpu.CompilerParams(dimension_semantics=sem, vmem_limit_bytes=VMEM_LIMIT_BYTES)


def _tiles(n, tile):
    assert n % tile == 0, (n, tile)
    return n // tile


def _dot(a, b):
    return jnp.dot(a.astype(BF16), b.astype(BF16), preferred_element_type=F32)


def _dot_nt(a, b):
    return lax.dot_general(a.astype(BF16), b.astype(BF16), (((1,), (1,)), ((), ())),
                           preferred_element_type=F32)


def _split_dot(x, m):
    xh = x.astype(BF16)
    xl = (x - xh.astype(F32)).astype(BF16)
    return jnp.dot(xh, m, preferred_element_type=F32) + jnp.dot(xl, m, preferred_element_type=F32)


def _split_dot_lhs(m, x):
    xh = x.astype(BF16)
    xl = (x - xh.astype(F32)).astype(BF16)
    return jnp.dot(m, xh, preferred_element_type=F32) + jnp.dot(m, xl, preferred_element_type=F32)


def _rms(x, g):
    return x * lax.rsqrt(jnp.mean(x * x, axis=-1, keepdims=True) + EPS_RMS) * g


def _in_proj_kernel(x_ref, xp_ref, xn_ref, g_ref, w_ref, gb_ref, mu_ref, w0_ref, wup_ref, a0_ref, aup_ref,
                    gup_ref, kk_ref, ka_ref, rk_ref, bd_ref,
                    u_ref, zc_ref, gt_ref, r_ref, v_ref, na_ref, g_out_ref, bon_ref, lw_ref, kd_ref, b_ref,
                    za_ref):
    i = pl.program_id(1)
    n = pl.num_programs(1)
    tm = x_ref.shape[1]
    hn = _rms(x_ref[0], g_ref[...]).astype(BF16)
    hp = jnp.where(i > 0, _rms(xp_ref[0], g_ref[...]), 0.0).astype(BF16)
    hx = jnp.where(i < n - 1, _rms(xn_ref[0], g_ref[...]), 0.0).astype(BF16)
    proj = lambda h, lo, width: jnp.dot(h, w_ref[:, lo:lo + width], preferred_element_type=F32)

    gate_lo = N_A_IN + 2 * D_B + D_C

    def gate_chunk(j):
        cols = slice(j * GATE_CHUNK, (j + 1) * GATE_CHUNK)
        gt_ref[0, :, cols] = jax.nn.sigmoid(
            proj(hn, gate_lo + j * GATE_CHUNK, GATE_CHUNK) + gb_ref[:, cols]).astype(BF16)

    zb = proj(hn, N_A_IN, 2 * D_B)
    za_ref[...] = proj(jnp.concatenate([hp, hn, hx], axis=0), 0, N_A_IN)
    u_ref[0] = zb[:, :D_B] * jax.nn.sigmoid(zb[:, D_B:])

    gate_chunk(0)
    z = za_ref[IN_HALO:IN_HALO + tm, :]
    zprev = za_ref[IN_HALO - 1:IN_HALO - 1 + tm, :]
    znext = za_ref[IN_HALO + 1:IN_HALO + 1 + tm, :]
    zs = mu_ref[0:1, :] * z + mu_ref[1:2, :] * zprev + mu_ref[2:3, :] * znext
    r = zs[:, 0:D_A]
    k = zs[:, D_A:2 * D_A]
    v = zs[:, 2 * D_A:3 * D_A]
    o = 3 * D_A
    tdw = jnp.tanh(zs[:, o:o + 2 * LORA]).astype(BF16)
    dab = zs[:, o + 2 * LORA:o + 4 * LORA].astype(BF16)
    dg = zs[:, o + 4 * LORA:o + 4 * LORA + LORA_G]
    bd = bd_ref[...]
    r_ref[0] = r
    v_ref[0] = v

    gate_chunk(1)
    g_out_ref[0] = _dot(jax.nn.sigmoid(dg), gup_ref[...])
    kk0 = k * kk_ref[...]
    nrm = jnp.sqrt(_dot(kk0 * kk0, bd))
    kk = kk0 / jnp.maximum(nrm, 1e-12)
    na_ref[0] = -kk
    ksum = jnp.zeros_like(k)
    for d in range(2):
        gate_chunk(2 + 2 * d)
        wl = w0_ref[d:d + 1, :] + jnp.dot(tdw, wup_ref[d], preferred_element_type=F32)
        lw_ref[0, d] = (-math.exp(-0.5)) * jax.nn.sigmoid(wl)
        a = jax.nn.sigmoid(a0_ref[d:d + 1, :] + jnp.dot(dab, aup_ref[d], preferred_element_type=F32))
        gate_chunk(3 + 2 * d)
        kd = k * (1.0 + (a - 1.0) * ka_ref[...])
        kd_ref[0, d] = kd
        b_ref[0, d] = kk * a
        ksum = ksum + kd
    zc_ref[0] = proj(hn, N_A_IN + 2 * D_B, D_C).astype(BF16)
    bon_ref[0] = _dot(r * rk_ref[...] * ksum, bd) * v


def _in_proj(x, lw, tm=256):
    b, s, _ = x.shape
    nh = tm // IN_HALO
    last = s // IN_HALO - 1
    const = lambda a: pl.BlockSpec(a.shape, lambda bi, i: (0,) * a.ndim, pipeline_mode=pl.Buffered(1))
    consts = (lw["norm_mix_g"], lw["w_in"], lw["gate_b"], lw["mu"], lw["w0"], lw["wup"], lw["a0"], lw["aup"],
              lw["gup"], lw["k_k"], lw["k_a"], lw["r_k"], lw["bd_ones"])
    tok = lambda width: pl.BlockSpec((1, tm, width), lambda bi, i: (bi, i, 0))
    dirs = pl.BlockSpec((1, 2, tm, D_A), lambda bi, i: (bi, 0, i, 0))
    tok_shape = lambda width, dt: jax.ShapeDtypeStruct((b, s, width), dt)
    dir_shape = jax.ShapeDtypeStruct((b, 2, s, D_A), F32)
    return pl.pallas_call(
        _in_proj_kernel,
        grid=(b, _tiles(s, tm)),
        in_specs=[tok(D_MODEL),
                  pl.BlockSpec((1, IN_HALO, D_MODEL), lambda bi, i: (bi, jnp.maximum(i * nh - 1, 0), 0)),
                  pl.BlockSpec((1, IN_HALO, D_MODEL), lambda bi, i: (bi, jnp.minimum((i + 1) * nh, last), 0)),
                  ] + [const(c) for c in consts],
        out_specs=[tok(D_B), tok(D_C), tok(3 * D_MODEL)] + [tok(D_A)] * 5 + [dirs] * 3,
        out_shape=[tok_shape(D_B, F32), tok_shape(D_C, BF16), tok_shape(3 * D_MODEL, BF16)]
        + [tok_shape(D_A, F32)] * 5 + [dir_shape] * 3,
        scratch_shapes=[pltpu.VMEM((tm + 2 * IN_HALO, N_A_IN), F32)],
        compiler_params=_params("parallel", "parallel"),
        name="in_proj",
    )(x, x, x, *consts)


def _wkv_masks(reverse):
    C, HW = WKV_CHUNK, WKV_HALF
    row = lax.broadcasted_iota(jnp.int32, (C, HW), 0)
    lane_i = lax.broadcasted_iota(jnp.int32, (C, HW), 1) & (C - 1)
    dt = (lane_i - row) if reverse else (row - lane_i)
    ti = lax.broadcasted_iota(jnp.int32, (C, C), 0)
    tj = lax.broadcasted_iota(jnp.int32, (C, C), 1)
    rr = lax.broadcasted_iota(jnp.int32, (HW, HW), 0)
    cc = lax.broadcasted_iota(jnp.int32, (HW, HW), 1)
    return dict(tri=jnp.where((tj >= ti) if reverse else (tj <= ti), 1.0, 0.0).astype(BF16),
                strict=dt > 0, incl=dt >= 0, eye=dt == 0,
                bd=(rr >> 6) == (cc >> 6), diag=rr == cc)


def _wkv_chunks(chains):
    C, HW = WKV_CHUNK, WKV_HALF
    n = range(len(chains))
    r, a, b, k, v, lw, sbd, m, rev = zip(*chains)
    bd = lambda i, x: jnp.where(m[i]["bd"], jnp.concatenate([x] * (HW // C), axis=0), 0.0)

    cum = [_split_dot_lhs(m[i]["tri"], lw[i]) for i in n]
    tot = [cum[i][0:1, :] if rev[i] else cum[i][C - 1:C, :] for i in n]
    e_inv = [jnp.exp(-cum[i]) for i in n]
    e_rem = [jnp.exp(tot[i] - cum[i]) for i in n]
    at = [a[i] * jnp.exp(cum[i] - lw[i]) for i in n]
    rt = [r[i] * jnp.exp(cum[i]) for i in n]

    gram = [_dot_nt(jnp.concatenate([at[i], rt[i]], axis=0),
                    jnp.concatenate([bd(i, b[i] * e_inv[i]), bd(i, k[i] * e_inv[i])], axis=0)) for i in n]
    l_ab = [jnp.where(m[i]["strict"], gram[i][:C, :HW], 0.0) for i in n]
    l_ak = [jnp.where(m[i]["strict"], gram[i][:C, HW:], 0.0) for i in n]
    m_rb = [jnp.where(m[i]["incl"], gram[i][C:, :HW], 0.0) for i in n]
    m_rk = [jnp.where(m[i]["incl"], gram[i][C:, HW:], 0.0) for i in n]

    t_inv = [jnp.where(m[i]["eye"], 1.0, 0.0) + l_ab[i] for i in n]
    lp = [_dot(l_ab[i], bd(i, l_ab[i])) for i in n]
    nsq = int(math.log2(C)) - 1
    for j in range(1, nsq):
        res = [_dot(jnp.concatenate([lp[i], t_inv[i]], axis=0), bd(i, lp[i])) for i in n]
        lp = [res[i][:C] for i in n]
        t_inv = [t_inv[i] + res[i][C:] for i in n]
    t_inv = [t_inv[i] + _dot(t_inv[i], bd(i, lp[i])) for i in n]

    lv = [_dot(jnp.concatenate([l_ak[i], m_rk[i]], axis=0), bd(i, v[i])) for i in n]
    pq = [_dot(t_inv[i], jnp.concatenate([bd(i, at[i]), bd(i, lv[i][:C])], axis=1)) for i in n]
    mpq = [_dot(m_rb[i], jnp.concatenate([bd(i, pq[i][:, :HW]), bd(i, pq[i][:, HW:])], axis=1)) for i in n]
    rp = [rt[i] + mpq[i][:, :HW] for i in n]
    yq = [mpq[i][:, HW:] + lv[i][C:] for i in n]

    o1 = [_dot(jnp.concatenate([pq[i][:, :HW], rp[i]], axis=0), sbd[i]) for i in n]
    lhs_t = [jnp.transpose(jnp.concatenate([b[i] * e_rem[i], k[i] * e_rem[i]], axis=0)) for i in n]
    upd = [_dot(lhs_t[i], jnp.concatenate([o1[i][:C] + pq[i][:, HW:], v[i]], axis=0)) for i in n]
    wc_col = [jnp.sum(jnp.where(m[i]["diag"], jnp.exp(tot[i]), 0.0), axis=1, keepdims=True) for i in n]
    return [(o1[i][C:] + yq[i], wc_col[i] * sbd[i] + jnp.where(m[i]["bd"], upd[i], 0.0)) for i in n]


def _wkv_kernel(rf_ref, vf_ref, af_ref, lwf_ref, kdf_ref, bf_ref,
                rb_ref, vb_ref, ab_ref, lwb_ref, kdb_ref, bb_ref, yf_ref, yb_ref, s_ref, *, nb):
    @pl.when(pl.program_id(1) == 0)
    def _():
        s_ref[...] = jnp.zeros_like(s_ref)

    dirs = ((rf_ref, vf_ref, af_ref, lwf_ref, kdf_ref, bf_ref, yf_ref),
            (rb_ref, vb_ref, ab_ref, lwb_ref, kdb_ref, bb_ref, yb_ref))
    masks = (_wkv_masks(False), _wkv_masks(True))
    chains, dests = [], []
    for bi in range(nb):
        for d, (r_ref, v_ref, a_ref, lw_ref, kd_ref, b_ref, y_ref) in enumerate(dirs):
            for h in range(D_A // WKV_HALF):
                sl = slice(h * WKV_HALF, (h + 1) * WKV_HALF)
                chains.append((r_ref[bi, :, sl], a_ref[bi, :, sl], b_ref[bi, 0, :, sl], kd_ref[bi, 0, :, sl],
                               v_ref[bi, :, sl], lw_ref[bi, 0, :, sl], s_ref[bi, d, h], masks[d], d == 1))
                dests.append((y_ref, bi, sl, d, h))
    for (y, s_new), (y_ref, bi, sl, d, h) in zip(_wkv_chunks(chains), dests):
        y_ref[bi, :, sl] = y
        s_ref[bi, d, h] = s_new


def _wkv(r, v, na, lw, kd, bb, nb=2):
    b, s, _ = r.shape
    nc = _tiles(s, WKV_CHUNK)
    tok_f = pl.BlockSpec((nb, WKV_CHUNK, D_A), lambda bi, c: (bi, c, 0))
    tok_b = pl.BlockSpec((nb, WKV_CHUNK, D_A), lambda bi, c: (bi, nc - 1 - c, 0))
    dir_f = pl.BlockSpec((nb, 1, WKV_CHUNK, D_A), lambda bi, c: (bi, 0, c, 0))
    dir_b = pl.BlockSpec((nb, 1, WKV_CHUNK, D_A), lambda bi, c: (bi, 1, nc - 1 - c, 0))
    y_shape = jax.ShapeDtypeStruct((b, s, D_A), F32)
    return pl.pallas_call(
        functools.partial(_wkv_kernel, nb=nb),
        grid=(_tiles(b, nb), nc),
        in_specs=[tok_f] * 3 + [dir_f] * 3 + [tok_b] * 3 + [dir_b] * 3,
        out_specs=[tok_f, tok_b],
        out_shape=[y_shape, y_shape],
        scratch_shapes=[pltpu.VMEM((nb, 2, D_A // WKV_HALF, WKV_HALF, WKV_HALF), F32)],
        compiler_params=_params("parallel", "arbitrary"),
        name="wkv_scan",
    )(r, v, na, lw, kd, bb, r, v, na, lw, kd, bb)


def _dft_kernel(x_ref, cw_ref, rev_ref, dm_ref, o_ref, rhs_ref, nyq_ref):
    i = pl.program_id(1)
    s = x_ref.shape[1]
    tm = o_ref.shape[1]
    half = s // 2
    nblk = s // DFT_BLK
    groups = [slice(g * GW_C, (g + 1) * GW_C) for g in range(D_C // GW_C)]

    @pl.when(i == 0)
    def _():
        for blk in range(half // DFT_BLK):
            rows = slice(blk * DFT_BLK, (blk + 1) * DFT_BLK)
            hi = x_ref[0, (nblk - blk - 1) * DFT_BLK:(nblk - blk) * DFT_BLK, :]
            if blk == 0:
                xr = jnp.dot(rev_ref[:, :DFT_BLK], hi, preferred_element_type=F32)
            else:
                top = x_ref[0, (nblk - blk) * DFT_BLK:(nblk - blk + 1) * DFT_BLK, :]
                xr = jnp.dot(rev_ref[...], jnp.concatenate([hi, top], axis=0), preferred_element_type=F32)
            xa = x_ref[0, rows, :].astype(F32)
            xe = (xa + xr).astype(BF16)
            xo = (xa - xr).astype(BF16)
            for sl in groups:
                rhs_ref[rows, sl] = jnp.dot(xe[:, sl], cw_ref[:, :GW_C], preferred_element_type=F32).astype(BF16)
                rhs_ref[half + blk * DFT_BLK:half + (blk + 1) * DFT_BLK, sl] = jnp.dot(
                    xo[:, sl], cw_ref[:, GW_C:], preferred_element_type=F32).astype(BF16)
        mid = x_ref[0, half:half + NYQ_ROWS, :]
        for sl in groups:
            nyq_ref[:, sl] = jnp.dot(mid[:, sl], cw_ref[:, :GW_C], preferred_element_type=F32) * s ** -0.5

    odd = (lax.broadcasted_iota(jnp.int32, (tm, D_C), 0) & 1) == 1
    nyq = nyq_ref[0:1, :]
    o_ref[0] = (jnp.dot(dm_ref[...], rhs_ref[...], preferred_element_type=F32)
                + jnp.where(odd, -nyq, nyq)).astype(o_ref.dtype)


def _dft(zc, cw, rev, dm, tm=512):
    b, s, _ = zc.shape
    assert tm % 2 == 0 and (s // 2) % DFT_BLK == 0
    return pl.pallas_call(
        _dft_kernel,
        grid=(b, _tiles(s, tm)),
        in_specs=[pl.BlockSpec((1, s, D_C), lambda bi, i: (bi, 0, 0)),
                  pl.BlockSpec(cw.shape, lambda bi, i: (0, 0)),
                  pl.BlockSpec(rev.shape, lambda bi, i: (0, 0)),
                  pl.BlockSpec((tm, s), lambda bi, i: (i, 0))],
        out_specs=pl.BlockSpec((1, tm, D_C), lambda bi, i: (bi, i, 0)),
        out_shape=jax.ShapeDtypeStruct((b, s, D_C), BF16),
        scratch_shapes=[pltpu.VMEM((s, D_C), BF16), pltpu.VMEM((NYQ_ROWS, D_C), F32)],
        compiler_params=_params("parallel", "arbitrary"),
        name="dft2_real",
    )(zc, cw, rev, dm)


def _dft_tables(s):
    def cs(n, cols):
        ang = ((jnp.arange(n, dtype=jnp.int32)[:, None] * jnp.arange(cols, dtype=jnp.int32)[None, :]) % n
               ).astype(F32) * (2.0 * math.pi / n)
        return jnp.cos(ang), jnp.sin(ang)
    cc, sc = cs(GW_C, GW_C)
    cw = (jnp.concatenate([cc, sc], axis=1) * GW_C ** -0.5).astype(BF16)
    j = jnp.arange(DFT_BLK, dtype=jnp.int32)[:, None]
    c = jnp.arange(2 * DFT_BLK, dtype=jnp.int32)[None, :]
    rev = (((j >= 1) & (c == DFT_BLK - j)) | ((j == 0) & (c == DFT_BLK))).astype(BF16)
    cp, sp = cs(s, s // 2)
    dm = (jnp.concatenate([cp, -sp], axis=1) * s ** -0.5).astype(BF16)
    return cw, rev, dm


def _merge_kernel(x_ref, yf_ref, yb_ref, bon_ref, g_ref, u_ref, up_ref, un_ref, f_ref, gt_ref,
                  lnxg_ref, lnxb_ref, cw_ref, cb_ref, clng_ref, clnb_ref, bdm_ref,
                  wa_ref, wb_ref, wc_ref, wo_ref, o_ref, us_ref, c_ref, *, sub):
    i = pl.program_id(1)
    n = pl.num_programs(1)
    tm = x_ref.shape[1]
    rows = tm + 2 * CONV_HALO

    bdm = bdm_ref[...]
    ys = yf_ref[0] + yb_ref[0]
    dev = ys - _split_dot(ys, bdm)
    var = _dot(dev * dev, bdm)
    yn = dev * lax.rsqrt(var + EPS_GN) * lnxg_ref[...] + lnxb_ref[...]
    ya = _dot((yn + bon_ref[0]) * g_ref[0], wa_ref[...])
    yc = _dot(f_ref[0], wc_ref[...])

    us_ref[0, 0:CONV_HALO, :] = jnp.where(i > 0, up_ref[0], 0.0)
    us_ref[0, CONV_HALO:CONV_HALO + tm, :] = u_ref[0]
    us_ref[0, CONV_HALO + tm:rows, :] = jnp.where(i < n - 1, un_ref[0], 0.0)
    for s in range(1, SUBLANES):
        us_ref[s, 0:rows - SUBLANES, :] = us_ref[0, s:s + rows - SUBLANES, :]
    first = CONV_HALO - CONV_K // 2
    for t0 in range(0, tm, sub):
        acc = jnp.zeros((sub, D_B), F32) + cb_ref[...]
        for j in range(CONV_K):
            q, s = divmod(first + j, SUBLANES)
            w = jnp.concatenate([cw_ref[j * SUBLANES:(j + 1) * SUBLANES, :]] * (sub // SUBLANES), axis=0)
            acc = acc + w * us_ref[s, t0 + q * SUBLANES:t0 + q * SUBLANES + sub, :]
        c_ref[t0:t0 + sub, :] = acc
    c = c_ref[...]
    cd = c - jnp.mean(c, axis=-1, keepdims=True)
    cn = cd * lax.rsqrt(jnp.mean(cd * cd, axis=-1, keepdims=True) + EPS_LN) * clng_ref[...] + clnb_ref[...]
    yb = _dot(cn * jax.nn.sigmoid(cn), wb_ref[...])

    gate = lambda j: gt_ref[0, :, j * D_MODEL:(j + 1) * D_MODEL].astype(F32)
    merged = gate(0) * ya + gate(1) * yb + gate(2) * yc
    o_ref[0] = x_ref[0] + _dot(merged, wo_ref[...])


def _merge(x, yf, yb, bon, g, u, f, gates, lw, tm=256, sub=32):
    b, s, _ = x.shape
    nh = tm // CONV_HALO
    last = s // CONV_HALO - 1
    const = lambda a: pl.BlockSpec(a.shape, lambda bi, i: (0,) * a.ndim, pipeline_mode=pl.Buffered(1))
    tok = lambda n: pl.BlockSpec((1, tm, n), lambda bi, i: (bi, i, 0))
    consts = (lw["lnx_g"], lw["lnx_b"], lw["conv_w"], lw["conv_b"], lw["conv_ln_g"], lw["conv_ln_b"],
              lw["bd_mean"], lw["w_a_out"], lw["w_b_out"], lw["w_c_out"], lw["w_out"])
    return pl.pallas_call(
        functools.partial(_merge_kernel, sub=sub),
        grid=(b, _tiles(s, tm)),
        in_specs=[tok(D_MODEL), tok(D_A), tok(D_A), tok(D_A), tok(D_A), tok(D_B),
                  pl.BlockSpec((1, CONV_HALO, D_B), lambda bi, i: (bi, jnp.maximum(i * nh - 1, 0), 0)),
                  pl.BlockSpec((1, CONV_HALO, D_B), lambda bi, i: (bi, jnp.minimum((i + 1) * nh, last), 0)),
                  tok(D_C), tok(3 * D_MODEL)] + [const(a) for a in consts],
        out_specs=tok(D_MODEL),
        out_shape=jax.ShapeDtypeStruct(x.shape, F32),
        scratch_shapes=[pltpu.VMEM((SUBLANES, tm + 2 * CONV_HALO, D_B), F32), pltpu.VMEM((tm, D_B), F32)],
        compiler_params=_params("parallel", "parallel"),
        name="merge",
    )(x, yf, yb, bon, g, u, u, u, f, gates, *consts)


def _norm_proj_kernel(x_ref, g_ref, w_ref, o_ref):
    o_ref[...] = _dot(_rms(x_ref[...], g_ref[...]), w_ref[...]).astype(o_ref.dtype)


def _norm_proj(x2, g, w, out_dtype, tm=256):
    t = x2.shape[0]
    n = w.shape[1]
    return pl.pallas_call(
        _norm_proj_kernel,
        grid=(_tiles(t, tm),),
        in_specs=[pl.BlockSpec((tm, D_MODEL), lambda i: (i, 0)),
                  pl.BlockSpec((1, D_MODEL), lambda i: (0, 0)),
                  pl.BlockSpec(w.shape, lambda i: (0, 0))],
        out_specs=pl.BlockSpec((tm, n), lambda i: (i, 0)),
        out_shape=jax.ShapeDtypeStruct((t, n), out_dtype),
        compiler_params=_params("parallel"),
        name="norm_proj",
    )(x2, g, w)


def _xattn_kernel(x_ref, kv_ref, g_ref, wq_ref, wo_ref, o_ref):
    x = x_ref[0]
    q = (_dot(_rms(x, g_ref[...]), wq_ref[...]) * HD_X ** -0.5).astype(BF16)
    heads = []
    for h in range(H_X):
        sl = slice(h * HD_X, (h + 1) * HD_X)
        sc = _dot_nt(q[:, sl], kv_ref[0, :, sl])
        e = jnp.exp(sc - jnp.max(sc, axis=-1, keepdims=True))
        p = e / jnp.sum(e, axis=-1, keepdims=True)
        heads.append(_dot(p, kv_ref[0, :, D_MODEL + h * HD_X:D_MODEL + (h + 1) * HD_X]))
    o_ref[0] = x + _dot(jnp.concatenate(heads, axis=1), wo_ref[...])


def _xattn(x, kv, lw, tm=1024):
    b, s, _ = x.shape
    full = lambda a: pl.BlockSpec(a.shape, lambda bi, i: (0,) * a.ndim)
    return pl.pallas_call(
        _xattn_kernel,
        grid=(b, _tiles(s, tm)),
        in_specs=[pl.BlockSpec((1, tm, D_MODEL), lambda bi, i: (bi, i, 0)),
                  pl.BlockSpec((1, N_MEM, 2 * D_MODEL), lambda bi, i: (bi, 0, 0)),
                  full(lw["norm_x_g"]), full(lw["wq"]), full(lw["wo"])],
        out_specs=pl.BlockSpec((1, tm, D_MODEL), lambda bi, i: (bi, i, 0)),
        out_shape=jax.ShapeDtypeStruct(x.shape, F32),
        compiler_params=_params("parallel", "parallel"),
        name="cross_attn",
    )(x, kv, lw["norm_x_g"], lw["wq"], lw["wo"])


def _mlp_kernel(x_ref, g_ref, w1_ref, w2_ref, fg_ref, o_ref, hn_ref, acc_ref, *, final):
    j = pl.program_id(1)

    @pl.when(j == 0)
    def _():
        hn_ref[...] = _rms(x_ref[...], g_ref[...]).astype(BF16)
        acc_ref[...] = jnp.zeros_like(acc_ref)

    h1 = jnp.maximum(jnp.dot(hn_ref[...], w1_ref[...], preferred_element_type=F32), 0.0)
    acc_ref[...] += _dot(h1 * h1, w2_ref[...])

    @pl.when(j == pl.num_programs(1) - 1)
    def _():
        y = x_ref[...] + acc_ref[...]
        o_ref[...] = _rms(y, fg_ref[...]) if final else y


def _mlp(x2, lw, final_g, final, tm=1024, tf=1024):
    t = x2.shape[0]
    return pl.pallas_call(
        functools.partial(_mlp_kernel, final=final),
        grid=(_tiles(t, tm), _tiles(D_FF, tf)),
        in_specs=[pl.BlockSpec((tm, D_MODEL), lambda i, j: (i, 0)),
                  pl.BlockSpec((1, D_MODEL), lambda i, j: (0, 0)),
                  pl.BlockSpec((D_MODEL, tf), lambda i, j: (0, j)),
                  pl.BlockSpec((tf, D_MODEL), lambda i, j: (j, 0)),
                  pl.BlockSpec((1, D_MODEL), lambda i, j: (0, 0))],
        out_specs=pl.BlockSpec((tm, D_MODEL), lambda i, j: (i, 0)),
        out_shape=jax.ShapeDtypeStruct((t, D_MODEL), F32),
        scratch_shapes=[pltpu.VMEM((tm, D_MODEL), BF16), pltpu.VMEM((tm, D_MODEL), F32)],
        compiler_params=_params("parallel", "arbitrary"),
        name="mlp",
    )(x2, lw["norm_mlp_g"], lw["w_mlp1"], lw["w_mlp2"], final_g)


def _stage_layer(p, l):
    row = lambda a: a.reshape(1, -1).astype(F32)
    pad_lora = lambda w: jnp.stack([
        jnp.zeros((2 * LORA, D_A), F32).at[d * LORA:(d + 1) * LORA].set(w[d]) for d in range(2)]).astype(BF16)
    hid = jnp.arange(D_A, dtype=jnp.int32) // HEAD_A
    same_head = (hid[:, None] == hid[None, :]).astype(F32)
    return dict(
        norm_mix_g=row(p["norm_mix_g"][l]), w_in=p["w_in"][l].astype(BF16),
        mu=jnp.concatenate([1.0 - p["shift_mu"][l][0:1] - p["shift_mu"][l][1:2], p["shift_mu"][l]], axis=0),
        w0=p["w0"][l], wup=pad_lora(p["w_up"][l]), a0=p["a0"][l],
        aup=pad_lora(p["a_up"][l]), gup=p["g_up"][l].astype(BF16),
        k_k=row(p["k_k"][l]), k_a=row(p["k_a"][l]), r_k=row(p["r_k"][l]),
        lnx_g=row(p["lnx_g"][l]), lnx_b=row(p["lnx_b"][l]),
        bd_ones=same_head.astype(BF16), bd_mean=(same_head / HEAD_A).astype(BF16),
        conv_w=jnp.repeat(p["conv_w"][l], SUBLANES, axis=0), conv_b=row(p["conv_b"][l]),
        conv_ln_g=row(p["conv_ln_g"][l]), conv_ln_b=row(p["conv_ln_b"][l]),
        gate_b=row(p["gate_b"][l]),
        w_a_out=p["w_a_out"][l].astype(BF16), w_b_out=p["w_b_out"][l].astype(BF16),
        w_c_out=p["w_c_out"][l].astype(BF16), w_out=p["w_out"][l].astype(BF16),
        norm_x_g=row(p["norm_x_g"][l]), norm_mem_g=row(p["norm_mem_g"][l]),
        wq=p["wq"][l].astype(BF16), wo=p["wo"][l].astype(BF16),
        wkv=jnp.concatenate([p["wk"][l], p["wv"][l]], axis=1).astype(BF16),
        norm_mlp_g=row(p["norm_mlp_g"][l]),
        w_mlp1=p["w_mlp1"][l].astype(BF16), w_mlp2=p["w_mlp2"][l].astype(BF16),
    )


def _trunk(x, mem, layers, final_g):
    b, s, _ = x.shape
    t = b * s
    cw, rev, dm = _dft_tables(s)
    mem2 = mem.reshape(b * N_MEM, D_MODEL)
    for l, lw in enumerate(layers):
        u, zc, gates, r, v, na, g, bon, lwd, kd, bb = _in_proj(x, lw)
        yf, yb = _wkv(r, v, na, lwd, kd, bb)
        f = _dft(zc, cw, rev, dm)
        x = _merge(x, yf, yb, bon, g, u, f, gates, lw)
        kv = _norm_proj(mem2, lw["norm_mem_g"], lw["wkv"], BF16).reshape(b, N_MEM, 2 * D_MODEL)
        x = _xattn(x, kv, lw)
        x = _mlp(x.reshape(t, D_MODEL), lw, final_g, l == len(layers) - 1).reshape(b, s, D_MODEL)
    return x


def kernel(x_prompt, x_sample, mem_prompt, mem_sample, norm_mix_g, w_in, shift_mu, w0, w_up, a0, a_up, g_up, k_k, k_a, r_k, lnx_g, lnx_b, w_a_out, conv_w, conv_b, conv_ln_g, conv_ln_b, w_b_out, w_c_out, gate_b, w_out, norm_x_g, norm_mem_g, wq, wk, wv, wo, norm_mlp_g, w_mlp1, w_mlp2, final_norm_g):
    p = dict(norm_mix_g=norm_mix_g, w_in=w_in, shift_mu=shift_mu, w0=w0, w_up=w_up, a0=a0,
             a_up=a_up, g_up=g_up, k_k=k_k, k_a=k_a, r_k=r_k, lnx_g=lnx_g, lnx_b=lnx_b,
             w_a_out=w_a_out, conv_w=conv_w, conv_b=conv_b, conv_ln_g=conv_ln_g,
             conv_ln_b=conv_ln_b, w_b_out=w_b_out, w_c_out=w_c_out, gate_b=gate_b, w_out=w_out,
             norm_x_g=norm_x_g, norm_mem_g=norm_mem_g, wq=wq, wk=wk, wv=wv, wo=wo,
             norm_mlp_g=norm_mlp_g, w_mlp1=w_mlp1, w_mlp2=w_mlp2)
    layers = [_stage_layer(p, l) for l in range(DEPTH)]
    final_g = final_norm_g.reshape(1, D_MODEL).astype(F32)
    return (_trunk(x_prompt, mem_prompt, layers, final_g), _trunk(x_sample, mem_sample, layers, final_g))
```

```python
import functools
import math

import jax
import jax.numpy as jnp
from jax import lax
from jax.experimental import pallas as pl
from jax.experimental.pallas import tpu as pltpu

F32 = jnp.float32
BF16 = jnp.bfloat16

D_MODEL = 1024
DEPTH = 4
N_MEM = 256
D_A = 512
HEAD_A = 64
LORA = 64
LORA_G = 128
N_A_IN = 3 * D_A + 4 * LORA + LORA_G
D_B = 512
CONV_K = 31
D_C = 512
GW_C = 128
H_X = 4
HD_X = D_MODEL // H_X
D_FF = 4 * D_MODEL

EPS_RMS = 1e-6
EPS_LN = 1e-5
EPS_GN = 64e-5

VMEM_LIMIT_BYTES = 56 * 1024 * 1024

WKV_CHUNK = 64
WKV_HALF = 128
CONV_HALO = 16
IN_HALO = 16
GATE_CHUNK = 512
SUBLANES = 8
DFT_BLK = 256
NYQ_ROWS = 16


def _params(*sem):
    return pltpu.CompilerParams(dimension_semantics=sem, vmem_limit_bytes=VMEM_LIMIT_BYTES)


def _tiles(n, tile):
    assert n % tile == 0, (n, tile)
    return n // tile


def _dot(a, b):
    return jnp.dot(a.astype(BF16), b.astype(BF16), preferred_element_type=F32)


def _dot_nt(a, b):
    return lax.dot_general(a.astype(BF16), b.astype(BF16), (((1,), (1,)), ((), ())),
                           preferred_element_type=F32)


def _split_dot(x, m):
    xh = x.astype(BF16)
    xl = (x - xh.astype(F32)).astype(BF16)
    return jnp.dot(xh, m, preferred_element_type=F32) + jnp.dot(xl, m, preferred_element_type=F32)


def _split_dot_lhs(m, x):
    xh = x.astype(BF16)
    xl = (x - xh.astype(F32)).astype(BF16)
    return jnp.dot(m, xh, preferred_element_type=F32) + jnp.dot(m, xl, preferred_element_type=F32)


def _rms(x, g):
    return x * lax.rsqrt(jnp.mean(x * x, axis=-1, keepdims=True) + EPS_RMS) * g


def _in_proj_kernel(x_ref, xp_ref, xn_ref, g_ref, w_ref, gb_ref, mu_ref, w0_ref, wup_ref, a0_ref, aup_ref,
                    gup_ref, kk_ref, ka_ref, rk_ref, bd_ref,
                    u_ref, zc_ref, gt_ref, r_ref, v_ref, na_ref, g_out_ref, bon_ref, lw_ref, kd_ref, b_ref,
                    za_ref):
    i = pl.program_id(1)
    n = pl.num_programs(1)
    tm = x_ref.shape[1]
    hn = _rms(x_ref[0], g_ref[...]).astype(BF16)
    hp = jnp.where(i > 0, _rms(xp_ref[0], g_ref[...]), 0.0).astype(BF16)
    hx = jnp.where(i < n - 1, _rms(xn_ref[0], g_ref[...]), 0.0).astype(BF16)
    proj = lambda h, lo, width: jnp.dot(h, w_ref[:, lo:lo + width], preferred_element_type=F32)

    gate_lo = N_A_IN + 2 * D_B + D_C

    def gate_chunk(j):
        cols = slice(j * GATE_CHUNK, (j + 1) * GATE_CHUNK)
        gt_ref[0, :, cols] = jax.nn.sigmoid(
            proj(hn, gate_lo + j * GATE_CHUNK, GATE_CHUNK) + gb_ref[:, cols]).astype(BF16)

    zb = proj(hn, N_A_IN, 2 * D_B)
    za_ref[...] = proj(jnp.concatenate([hp, hn, hx], axis=0), 0, N_A_IN)
    u_ref[0] = zb[:, :D_B] * jax.nn.sigmoid(zb[:, D_B:])

    gate_chunk(0)
    z = za_ref[IN_HALO:IN_HALO + tm, :]
    zprev = za_ref[IN_HALO - 1:IN_HALO - 1 + tm, :]
    znext = za_ref[IN_HALO + 1:IN_HALO + 1 + tm, :]
    zs = mu_ref[0:1, :] * z + mu_ref[1:2, :] * zprev + mu_ref[2:3, :] * znext
    r = zs[:, 0:D_A]
    k = zs[:, D_A:2 * D_A]
    v = zs[:, 2 * D_A:3 * D_A]
    o = 3 * D_A
    tdw = jnp.tanh(zs[:, o:o + 2 * LORA]).astype(BF16)
    dab = zs[:, o + 2 * LORA:o + 4 * LORA].astype(BF16)
    dg = zs[:, o + 4 * LORA:o + 4 * LORA + LORA_G]
    bd = bd_ref[...]
    r_ref[0] = r
    v_ref[0] = v

    gate_chunk(1)
    g_out_ref[0] = _dot(jax.nn.sigmoid(dg), gup_ref[...])
    kk0 = k * kk_ref[...]
    nrm = jnp.sqrt(_dot(kk0 * kk0, bd))
    kk = kk0 / jnp.maximum(nrm, 1e-12)
    na_ref[0] = -kk
    ksum = jnp.zeros_like(k)
    for d in range(2):
        gate_chunk(2 + 2 * d)
        wl = w0_ref[d:d + 1, :] + jnp.dot(tdw, wup_ref[d], preferred_element_type=F32)
        lw_ref[0, d] = (-math.exp(-0.5)) * jax.nn.sigmoid(wl)
        a = jax.nn.sigmoid(a0_ref[d:d + 1, :] + jnp.dot(dab, aup_ref[d], preferred_element_type=F32))
        gate_chunk(3 + 2 * d)
        kd = k * (1.0 + (a - 1.0) * ka_ref[...])
        kd_ref[0, d] = kd
        b_ref[0, d] = kk * a
        ksum = ksum + kd
    zc_ref[0] = proj(hn, N_A_IN + 2 * D_B, D_C).astype(BF16)
    bon_ref[0] = _dot(r * rk_ref[...] * ksum, bd) * v


def _in_proj(x, lw, tm=256):
    b, s, _ = x.shape
    nh = tm // IN_HALO
    last = s // IN_HALO - 1
    const = lambda a: pl.BlockSpec(a.shape, lambda bi, i: (0,) * a.ndim, pipeline_mode=pl.Buffered(1))
    consts = (lw["norm_mix_g"], lw["w_in"], lw["gate_b"], lw["mu"], lw["w0"], lw["wup"], lw["a0"], lw["aup"],
              lw["gup"], lw["k_k"], lw["k_a"], lw["r_k"], lw["bd_ones"])
    tok = lambda width: pl.BlockSpec((1, tm, width), lambda bi, i: (bi, i, 0))
    dirs = pl.BlockSpec((1, 2, tm, D_A), lambda bi, i: (bi, 0, i, 0))
    tok_shape = lambda width, dt: jax.ShapeDtypeStruct((b, s, width), dt)
    dir_shape = jax.ShapeDtypeStruct((b, 2, s, D_A), F32)
    return pl.pallas_call(
        _in_proj_kernel,
        grid=(b, _tiles(s, tm)),
        in_specs=[tok(D_MODEL),
                  pl.BlockSpec((1, IN_HALO, D_MODEL), lambda bi, i: (bi, jnp.maximum(i * nh - 1, 0), 0)),
                  pl.BlockSpec((1, IN_HALO, D_MODEL), lambda bi, i: (bi, jnp.minimum((i + 1) * nh, last), 0)),
                  ] + [const(c) for c in consts],
        out_specs=[tok(D_B), tok(D_C), tok(3 * D_MODEL)] + [tok(D_A)] * 5 + [dirs] * 3,
        out_shape=[tok_shape(D_B, F32), tok_shape(D_C, BF16), tok_shape(3 * D_MODEL, BF16)]
        + [tok_shape(D_A, F32)] * 5 + [dir_shape] * 3,
        scratch_shapes=[pltpu.VMEM((tm + 2 * IN_HALO, N_A_IN), F32)],
        compiler_params=_params("parallel", "parallel"),
        name="in_proj",
    )(x, x, x, *consts)


def _wkv_masks(reverse):
    C, HW = WKV_CHUNK, WKV_HALF
    row = lax.broadcasted_iota(jnp.int32, (C, HW), 0)
    lane_i = lax.broadcasted_iota(jnp.int32, (C, HW), 1) & (C - 1)
    dt = (lane_i - row) if reverse else (row - lane_i)
    ti = lax.broadcasted_iota(jnp.int32, (C, C), 0)
    tj = lax.broadcasted_iota(jnp.int32, (C, C), 1)
    rr = lax.broadcasted_iota(jnp.int32, (HW, HW), 0)
    cc = lax.broadcasted_iota(jnp.int32, (HW, HW), 1)
    return dict(tri=jnp.where((tj >= ti) if reverse else (tj <= ti), 1.0, 0.0).astype(BF16),
                strict=dt > 0, incl=dt >= 0, eye=dt == 0,
                bd=(rr >> 6) == (cc >> 6), diag=rr == cc)


def _wkv_chunks(chains):
    C, HW = WKV_CHUNK, WKV_HALF
    n = range(len(chains))
    r, a, b, k, v, lw, sbd, m, rev = zip(*chains)
    bd = lambda i, x: jnp.where(m[i]["bd"], jnp.concatenate([x] * (HW // C), axis=0), 0.0)

    cum = [_split_dot_lhs(m[i]["tri"], lw[i]) for i in n]
    tot = [cum[i][0:1, :] if rev[i] else cum[i][C - 1:C, :] for i in n]
    e_inv = [jnp.exp(-cum[i]) for i in n]
    e_rem = [jnp.exp(tot[i] - cum[i]) for i in n]
    at = [a[i] * jnp.exp(cum[i] - lw[i]) for i in n]
    rt = [r[i] * jnp.exp(cum[i]) for i in n]

    gram = [_dot_nt(jnp.concatenate([at[i], rt[i]], axis=0),
                    jnp.concatenate([bd(i, b[i] * e_inv[i]), bd(i, k[i] * e_inv[i])], axis=0)) for i in n]
    l_ab = [jnp.where(m[i]["strict"], gram[i][:C, :HW], 0.0) for i in n]
    l_ak = [jnp.where(m[i]["strict"], gram[i][:C, HW:], 0.0) for i in n]
    m_rb = [jnp.where(m[i]["incl"], gram[i][C:, :HW], 0.0) for i in n]
    m_rk = [jnp.where(m[i]["incl"], gram[i][C:, HW:], 0.0) for i in n]

    t_inv = [jnp.where(m[i]["eye"], 1.0, 0.0) + l_ab[i] for i in n]
    lp = [_dot(l_ab[i], bd(i, l_ab[i])) for i in n]
    nsq = int(math.log2(C)) - 1
    for j in range(1, nsq):
        res = [_dot(jnp.concatenate([lp[i], t_inv[i]], axis=0), bd(i, lp[i])) for i in n]
        lp = [res[i][:C] for i in n]
        t_inv = [t_inv[i] + res[i][C:] for i in n]
    t_inv = [t_inv[i] + _dot(t_inv[i], bd(i, lp[i])) for i in n]

    lv = [_dot(jnp.concatenate([l_ak[i], m_rk[i]], axis=0), bd(i, v[i])) for i in n]
    pq = [_dot(t_inv[i], jnp.concatenate([bd(i, at[i]), bd(i, lv[i][:C])], axis=1)) for i in n]
    mpq = [_dot(m_rb[i], jnp.concatenate([bd(i, pq[i][:, :HW]), bd(i, pq[i][:, HW:])], axis=1)) for i in n]
    rp = [rt[i] + mpq[i][:, :HW] for i in n]
    yq = [mpq[i][:, HW:] + lv[i][C:] for i in n]

    o1 = [_dot(jnp.concatenate([pq[i][:, :HW], rp[i]], axis=0), sbd[i]) for i in n]
    lhs_t = [jnp.transpose(jnp.concatenate([b[i] * e_rem[i], k[i] * e_rem[i]], axis=0)) for i in n]
    upd = [_dot(lhs_t[i], jnp.concatenate([o1[i][:C] + pq[i][:, HW:], v[i]], axis=0)) for i in n]
    wc_col = [jnp.sum(jnp.where(m[i]["diag"], jnp.exp(tot[i]), 0.0), axis=1, keepdims=True) for i in n]
    return [(o1[i][C:] + yq[i], wc_col[i] * sbd[i] + jnp.where(m[i]["bd"], upd[i], 0.0)) for i in n]


def _wkv_kernel(rf_ref, vf_ref, af_ref, lwf_ref, kdf_ref, bf_ref,
                rb_ref, vb_ref, ab_ref, lwb_ref, kdb_ref, bb_ref, yf_ref, yb_ref, s_ref, *, nb):
    @pl.when(pl.program_id(1) == 0)
    def _():
        s_ref[...] = jnp.zeros_like(s_ref)

    dirs = ((rf_ref, vf_ref, af_ref, lwf_ref, kdf_ref, bf_ref, yf_ref),
            (rb_ref, vb_ref, ab_ref, lwb_ref, kdb_ref, bb_ref, yb_ref))
    masks = (_wkv_masks(False), _wkv_masks(True))
    chains, dests = [], []
    for bi in range(nb):
        for d, (r_ref, v_ref, a_ref, lw_ref, kd_ref, b_ref, y_ref) in enumerate(dirs):
            for h in range(D_A // WKV_HALF):
                sl = slice(h * WKV_HALF, (h + 1) * WKV_HALF)
                chains.append((r_ref[bi, :, sl], a_ref[bi, :, sl], b_ref[bi, 0, :, sl], kd_ref[bi, 0, :, sl],
                               v_ref[bi, :, sl], lw_ref[bi, 0, :, sl], s_ref[bi, d, h], masks[d], d == 1))
                dests.append((y_ref, bi, sl, d, h))
    for (y, s_new), (y_ref, bi, sl, d, h) in zip(_wkv_chunks(chains), dests):
        y_ref[bi, :, sl] = y
        s_ref[bi, d, h] = s_new


def _wkv(r, v, na, lw, kd, bb, nb=4):
    b, s, _ = r.shape
    nc = _tiles(s, WKV_CHUNK)
    tok_f = pl.BlockSpec((nb, WKV_CHUNK, D_A), lambda bi, c: (bi, c, 0))
    tok_b = pl.BlockSpec((nb, WKV_CHUNK, D_A), lambda bi, c: (bi, nc - 1 - c, 0))
    dir_f = pl.BlockSpec((nb, 1, WKV_CHUNK, D_A), lambda bi, c: (bi, 0, c, 0))
    dir_b = pl.BlockSpec((nb, 1, WKV_CHUNK, D_A), lambda bi, c: (bi, 1, nc - 1 - c, 0))
    y_shape = jax.ShapeDtypeStruct((b, s, D_A), F32)
    return pl.pallas_call(
        functools.partial(_wkv_kernel, nb=nb),
        grid=(_tiles(b, nb), nc),
        in_specs=[tok_f] * 3 + [dir_f] * 3 + [tok_b] * 3 + [dir_b] * 3,
        out_specs=[tok_f, tok_b],
        out_shape=[y_shape, y_shape],
        scratch_shapes=[pltpu.VMEM((nb, 2, D_A // WKV_HALF, WKV_HALF, WKV_HALF), F32)],
        compiler_params=_params("parallel", "arbitrary"),
        name="wkv_scan",
    )(r, v, na, lw, kd, bb, r, v, na, lw, kd, bb)


def _dft_kernel(x_ref, cw_ref, rev_ref, dm_ref, o_ref, rhs_ref, nyq_ref):
    i = pl.program_id(1)
    s = x_ref.shape[1]
    tm = o_ref.shape[1]
    half = s // 2
    nblk = s // DFT_BLK
    groups = [slice(g * GW_C, (g + 1) * GW_C) for g in range(D_C // GW_C)]

    @pl.when(i == 0)
    def _():
        for blk in range(half // DFT_BLK):
            rows = slice(blk * DFT_BLK, (blk + 1) * DFT_BLK)
            hi = x_ref[0, (nblk - blk - 1) * DFT_BLK:(nblk - blk) * DFT_BLK, :]
            if blk == 0:
                xr = jnp.dot(rev_ref[:, :DFT_BLK], hi, preferred_element_type=F32)
            else:
                top = x_ref[0, (nblk - blk) * DFT_BLK:(nblk - blk + 1) * DFT_BLK, :]
                xr = jnp.dot(rev_ref[...], jnp.concatenate([hi, top], axis=0), preferred_element_type=F32)
            xa = x_ref[0, rows, :].astype(F32)
            xe = (xa + xr).astype(BF16)
            xo = (xa - xr).astype(BF16)
            for sl in groups:
                rhs_ref[rows, sl] = jnp.dot(xe[:, sl], cw_ref[:, :GW_C], preferred_element_type=F32).astype(BF16)
                rhs_ref[half + blk * DFT_BLK:half + (blk + 1) * DFT_BLK, sl] = jnp.dot(
                    xo[:, sl], cw_ref[:, GW_C:], preferred_element_type=F32).astype(BF16)
        mid = x_ref[0, half:half + NYQ_ROWS, :]
        for sl in groups:
            nyq_ref[:, sl] = jnp.dot(mid[:, sl], cw_ref[:, :GW_C], preferred_element_type=F32) * s ** -0.5

    odd = (lax.broadcasted_iota(jnp.int32, (tm, D_C), 0) & 1) == 1
    nyq = nyq_ref[0:1, :]
    o_ref[0] = (jnp.dot(dm_ref[...], rhs_ref[...], preferred_element_type=F32)
                + jnp.where(odd, -nyq, nyq)).astype(o_ref.dtype)


def _dft(zc, cw, rev, dm, tm=512):
    b, s, _ = zc.shape
    assert tm % 2 == 0 and (s // 2) % DFT_BLK == 0
    return pl.pallas_call(
        _dft_kernel,
        grid=(b, _tiles(s, tm)),
        in_specs=[pl.BlockSpec((1, s, D_C), lambda bi, i: (bi, 0, 0)),
                  pl.BlockSpec(cw.shape, lambda bi, i: (0, 0)),
                  pl.BlockSpec(rev.shape, lambda bi, i: (0, 0)),
                  pl.BlockSpec((tm, s), lambda bi, i: (i, 0))],
        out_specs=pl.BlockSpec((1, tm, D_C), lambda bi, i: (bi, i, 0)),
        out_shape=jax.ShapeDtypeStruct((b, s, D_C), BF16),
        scratch_shapes=[pltpu.VMEM((s, D_C), BF16), pltpu.VMEM((NYQ_ROWS, D_C), F32)],
        compiler_params=_params("parallel", "arbitrary"),
        name="dft2_real",
    )(zc, cw, rev, dm)


def _dft_tables(s):
    def cs(n, cols):
        ang = ((jnp.arange(n, dtype=jnp.int32)[:, None] * jnp.arange(cols, dtype=jnp.int32)[None, :]) % n
               ).astype(F32) * (2.0 * math.pi / n)
        return jnp.cos(ang), jnp.sin(ang)
    cc, sc = cs(GW_C, GW_C)
    cw = (jnp.concatenate([cc, sc], axis=1) * GW_C ** -0.5).astype(BF16)
    j = jnp.arange(DFT_BLK, dtype=jnp.int32)[:, None]
    c = jnp.arange(2 * DFT_BLK, dtype=jnp.int32)[None, :]
    rev = (((j >= 1) & (c == DFT_BLK - j)) | ((j == 0) & (c == DFT_BLK))).astype(BF16)
    cp, sp = cs(s, s // 2)
    dm = (jnp.concatenate([cp, -sp], axis=1) * s ** -0.5).astype(BF16)
    return cw, rev, dm


def _merge_kernel(x_ref, yf_ref, yb_ref, bon_ref, g_ref, u_ref, up_ref, un_ref, f_ref, gt_ref,
                  lnxg_ref, lnxb_ref, cw_ref, cb_ref, clng_ref, clnb_ref, bdm_ref,
                  wa_ref, wb_ref, wc_ref, wo_ref, o_ref, us_ref, c_ref, *, sub):
    i = pl.program_id(1)
    n = pl.num_programs(1)
    tm = x_ref.shape[1]
    rows = tm + 2 * CONV_HALO

    bdm = bdm_ref[...]
    ys = yf_ref[0] + yb_ref[0]
    dev = ys - _split_dot(ys, bdm)
    var = _dot(dev * dev, bdm)
    yn = dev * lax.rsqrt(var + EPS_GN) * lnxg_ref[...] + lnxb_ref[...]
    ya = _dot((yn + bon_ref[0]) * g_ref[0], wa_ref[...])
    yc = _dot(f_ref[0], wc_ref[...])

    us_ref[0, 0:CONV_HALO, :] = jnp.where(i > 0, up_ref[0], 0.0)
    us_ref[0, CONV_HALO:CONV_HALO + tm, :] = u_ref[0]
    us_ref[0, CONV_HALO + tm:rows, :] = jnp.where(i < n - 1, un_ref[0], 0.0)
    for s in range(1, SUBLANES):
        us_ref[s, 0:rows - SUBLANES, :] = us_ref[0, s:s + rows - SUBLANES, :]
    first = CONV_HALO - CONV_K // 2
    for t0 in range(0, tm, sub):
        acc = jnp.zeros((sub, D_B), F32) + cb_ref[...]
        for j in range(CONV_K):
            q, s = divmod(first + j, SUBLANES)
            w = jnp.concatenate([cw_ref[j * SUBLANES:(j + 1) * SUBLANES, :]] * (sub // SUBLANES), axis=0)
            acc = acc + w * us_ref[s, t0 + q * SUBLANES:t0 + q * SUBLANES + sub, :]
        c_ref[t0:t0 + sub, :] = acc
    c = c_ref[...]
    cd = c - jnp.mean(c, axis=-1, keepdims=True)
    cn = cd * lax.rsqrt(jnp.mean(cd * cd, axis=-1, keepdims=True) + EPS_LN) * clng_ref[...] + clnb_ref[...]
    yb = _dot(cn * jax.nn.sigmoid(cn), wb_ref[...])

    gate = lambda j: gt_ref[0, :, j * D_MODEL:(j + 1) * D_MODEL].astype(F32)
    merged = gate(0) * ya + gate(1) * yb + gate(2) * yc
    o_ref[0] = x_ref[0] + _dot(merged, wo_ref[...])


def _merge(x, yf, yb, bon, g, u, f, gates, lw, tm=256, sub=32):
    b, s, _ = x.shape
    nh = tm // CONV_HALO
    last = s // CONV_HALO - 1
    const = lambda a: pl.BlockSpec(a.shape, lambda bi, i: (0,) * a.ndim, pipeline_mode=pl.Buffered(1))
    tok = lambda n: pl.BlockSpec((1, tm, n), lambda bi, i: (bi, i, 0))
    consts = (lw["lnx_g"], lw["lnx_b"], lw["conv_w"], lw["conv_b"], lw["conv_ln_g"], lw["conv_ln_b"],
              lw["bd_mean"], lw["w_a_out"], lw["w_b_out"], lw["w_c_out"], lw["w_out"])
    return pl.pallas_call(
        functools.partial(_merge_kernel, sub=sub),
        grid=(b, _tiles(s, tm)),
        in_specs=[tok(D_MODEL), tok(D_A), tok(D_A), tok(D_A), tok(D_A), tok(D_B),
                  pl.BlockSpec((1, CONV_HALO, D_B), lambda bi, i: (bi, jnp.maximum(i * nh - 1, 0), 0)),
                  pl.BlockSpec((1, CONV_HALO, D_B), lambda bi, i: (bi, jnp.minimum((i + 1) * nh, last), 0)),
                  tok(D_C), tok(3 * D_MODEL)] + [const(a) for a in consts],
        out_specs=tok(D_MODEL),
        out_shape=jax.ShapeDtypeStruct(x.shape, F32),
        scratch_shapes=[pltpu.VMEM((SUBLANES, tm + 2 * CONV_HALO, D_B), F32), pltpu.VMEM((tm, D_B), F32)],
        compiler_params=_params("parallel", "parallel"),
        name="merge",
    )(x, yf, yb, bon, g, u, u, u, f, gates, *consts)


def _norm_proj_kernel(x_ref, g_ref, w_ref, o_ref):
    o_ref[...] = _dot(_rms(x_ref[...], g_ref[...]), w_ref[...]).astype(o_ref.dtype)


def _norm_proj(x2, g, w, out_dtype, tm=256):
    t = x2.shape[0]
    n = w.shape[1]
    return pl.pallas_call(
        _norm_proj_kernel,
        grid=(_tiles(t, tm),),
        in_specs=[pl.BlockSpec((tm, D_MODEL), lambda i: (i, 0)),
                  pl.BlockSpec((1, D_MODEL), lambda i: (0, 0)),
                  pl.BlockSpec(w.shape, lambda i: (0, 0))],
        out_specs=pl.BlockSpec((tm, n), lambda i: (i, 0)),
        out_shape=jax.ShapeDtypeStruct((t, n), out_dtype),
        compiler_params=_params("parallel"),
        name="norm_proj",
    )(x2, g, w)


def _xattn_kernel(x_ref, kv_ref, g_ref, wq_ref, wo_ref, o_ref):
    x = x_ref[0]
    q = (_dot(_rms(x, g_ref[...]), wq_ref[...]) * HD_X ** -0.5).astype(BF16)
    heads = []
    for h in range(H_X):
        sl = slice(h * HD_X, (h + 1) * HD_X)
        sc = _dot_nt(q[:, sl], kv_ref[0, :, sl])
        e = jnp.exp(sc - jnp.max(sc, axis=-1, keepdims=True))
        p = e / jnp.sum(e, axis=-1, keepdims=True)
        heads.append(_dot(p, kv_ref[0, :, D_MODEL + h * HD_X:D_MODEL + (h + 1) * HD_X]))
    o_ref[0] = x + _dot(jnp.concatenate(heads, axis=1), wo_ref[...])


def _xattn(x, kv, lw, tm=1024):
    b, s, _ = x.shape
    full = lambda a: pl.BlockSpec(a.shape, lambda bi, i: (0,) * a.ndim)
    return pl.pallas_call(
        _xattn_kernel,
        grid=(b, _tiles(s, tm)),
        in_specs=[pl.BlockSpec((1, tm, D_MODEL), lambda bi, i: (bi, i, 0)),
                  pl.BlockSpec((1, N_MEM, 2 * D_MODEL), lambda bi, i: (bi, 0, 0)),
                  full(lw["norm_x_g"]), full(lw["wq"]), full(lw["wo"])],
        out_specs=pl.BlockSpec((1, tm, D_MODEL), lambda bi, i: (bi, i, 0)),
        out_shape=jax.ShapeDtypeStruct(x.shape, F32),
        compiler_params=_params("parallel", "parallel"),
        name="cross_attn",
    )(x, kv, lw["norm_x_g"], lw["wq"], lw["wo"])


def _mlp_kernel(x_ref, g_ref, w1_ref, w2_ref, fg_ref, o_ref, hn_ref, acc_ref, *, final):
    j = pl.program_id(1)

    @pl.when(j == 0)
    def _():
        hn_ref[...] = _rms(x_ref[...], g_ref[...]).astype(BF16)
        acc_ref[...] = jnp.zeros_like(acc_ref)

    h1 = jnp.maximum(jnp.dot(hn_ref[...], w1_ref[...], preferred_element_type=F32), 0.0)
    acc_ref[...] += _dot(h1 * h1, w2_ref[...])

    @pl.when(j == pl.num_programs(1) - 1)
    def _():
        y = x_ref[...] + acc_ref[...]
        o_ref[...] = _rms(y, fg_ref[...]) if final else y


def _mlp(x2, lw, final_g, final, tm=1024, tf=1024):
    t = x2.shape[0]
    return pl.pallas_call(
        functools.partial(_mlp_kernel, final=final),
        grid=(_tiles(t, tm), _tiles(D_FF, tf)),
        in_specs=[pl.BlockSpec((tm, D_MODEL), lambda i, j: (i, 0)),
                  pl.BlockSpec((1, D_MODEL), lambda i, j: (0, 0)),
                  pl.BlockSpec((D_MODEL, tf), lambda i, j: (0, j)),
                  pl.BlockSpec((tf, D_MODEL), lambda i, j: (j, 0)),
                  pl.BlockSpec((1, D_MODEL), lambda i, j: (0, 0))],
        out_specs=pl.BlockSpec((tm, D_MODEL), lambda i, j: (i, 0)),
        out_shape=jax.ShapeDtypeStruct((t, D_MODEL), F32),
        scratch_shapes=[pltpu.VMEM((tm, D_MODEL), BF16), pltpu.VMEM((tm, D_MODEL), F32)],
        compiler_params=_params("parallel", "arbitrary"),
        name="mlp",
    )(x2, lw["norm_mlp_g"], lw["w_mlp1"], lw["w_mlp2"], final_g)


def _stage_layer(p, l):
    row = lambda a: a.reshape(1, -1).astype(F32)
    pad_lora = lambda w: jnp.stack([
        jnp.zeros((2 * LORA, D_A), F32).at[d * LORA:(d + 1) * LORA].set(w[d]) for d in range(2)]).astype(BF16)
    hid = jnp.arange(D_A, dtype=jnp.int32) // HEAD_A
    same_head = (hid[:, None] == hid[None, :]).astype(F32)
    return dict(
        norm_mix_g=row(p["norm_mix_g"][l]), w_in=p["w_in"][l].astype(BF16),
        mu=jnp.concatenate([1.0 - p["shift_mu"][l][0:1] - p["shift_mu"][l][1:2], p["shift_mu"][l]], axis=0),
        w0=p["w0"][l], wup=pad_lora(p["w_up"][l]), a0=p["a0"][l],
        aup=pad_lora(p["a_up"][l]), gup=p["g_up"][l].astype(BF16),
        k_k=row(p["k_k"][l]), k_a=row(p["k_a"][l]), r_k=row(p["r_k"][l]),
        lnx_g=row(p["lnx_g"][l]), lnx_b=row(p["lnx_b"][l]),
        bd_ones=same_head.astype(BF16), bd_mean=(same_head / HEAD_A).astype(BF16),
        conv_w=jnp.repeat(p["conv_w"][l], SUBLANES, axis=0), conv_b=row(p["conv_b"][l]),
        conv_ln_g=row(p["conv_ln_g"][l]), conv_ln_b=row(p["conv_ln_b"][l]),
        gate_b=row(p["gate_b"][l]),
        w_a_out=p["w_a_out"][l].astype(BF16), w_b_out=p["w_b_out"][l].astype(BF16),
        w_c_out=p["w_c_out"][l].astype(BF16), w_out=p["w_out"][l].astype(BF16),
        norm_x_g=row(p["norm_x_g"][l]), norm_mem_g=row(p["norm_mem_g"][l]),
        wq=p["wq"][l].astype(BF16), wo=p["wo"][l].astype(BF16),
        wkv=jnp.concatenate([p["wk"][l], p["wv"][l]], axis=1).astype(BF16),
        norm_mlp_g=row(p["norm_mlp_g"][l]),
        w_mlp1=p["w_mlp1"][l].astype(BF16), w_mlp2=p["w_mlp2"][l].astype(BF16),
    )


def _trunk(x, mem, layers, final_g):
    b, s, _ = x.shape
    t = b * s
    cw, rev, dm = _dft_tables(s)
    mem2 = mem.reshape(b * N_MEM, D_MODEL)
    for l, lw in enumerate(layers):
        u, zc, gates, r, v, na, g, bon, lwd, kd, bb = _in_proj(x, lw)
        yf, yb = _wkv(r, v, na, lwd, kd, bb)
        f = _dft(zc, cw, rev, dm)
        x = _merge(x, yf, yb, bon, g, u, f, gates, lw)
        kv = _norm_proj(mem2, lw["norm_mem_g"], lw["wkv"], BF16).reshape(b, N_MEM, 2 * D_MODEL)
        x = _xattn(x, kv, lw)
        x = _mlp(x.reshape(t, D_MODEL), lw, final_g, l == len(layers) - 1).reshape(b, s, D_MODEL)
    return x


def kernel(x_prompt, x_sample, mem_prompt, mem_sample, norm_mix_g, w_in, shift_mu, w0, w_up, a0, a_up, g_up, k_k, k_a, r_k, lnx_g, lnx_b, w_a_out, conv_w, conv_b, conv_ln_g, conv_ln_b, w_b_out, w_c_out, gate_b, w_out, norm_x_g, norm_mem_g, wq, wk, wv, wo, norm_mlp_g, w_mlp1, w_mlp2, final_norm_g):
    p = dict(norm_mix_g=norm_mix_g, w_in=w_in, shift_mu=shift_mu, w0=w0, w_up=w_up, a0=a0,
             a_up=a_up, g_up=g_up, k_k=k_k, k_a=k_a, r_k=r_k, lnx_g=lnx_g, lnx_b=lnx_b,
             w_a_out=w_a_out, conv_w=conv_w, conv_b=conv_b, conv_ln_g=conv_ln_g,
             conv_ln_b=conv_ln_b, w_b_out=w_b_out, w_c_out=w_c_out, gate_b=gate_b, w_out=w_out,
             norm_x_g=norm_x_g, norm_mem_g=norm_mem_g, wq=wq, wk=wk, wv=wv, wo=wo,
             norm_mlp_g=norm_mlp_g, w_mlp1=w_mlp1, w_mlp2=w_mlp2)
    layers = [_stage_layer(p, l) for l in range(DEPTH)]
    final_g = final_norm_g.reshape(1, D_MODEL).astype(F32)
    return (_trunk(x_prompt, mem_prompt, layers, final_g), _trunk(x_sample, mem_sample, layers, final_g))
```

```python
import functools
import math

import jax
import jax.numpy as jnp
from jax import lax
from jax.experimental import pallas as pl
from jax.experimental.pallas import tpu as pltpu

F32 = jnp.float32
BF16 = jnp.bfloat16

D_MODEL = 1024
DEPTH = 4
N_MEM = 256
D_A = 512
HEAD_A = 64
LORA = 64
LORA_G = 128
N_A_IN = 3 * D_A + 4 * LORA + LORA_G
D_B = 512
CONV_K = 31
D_C = 512
GW_C = 128
H_X = 4
HD_X = D_MODEL // H_X
D_FF = 4 * D_MODEL

EPS_RMS = 1e-6
EPS_LN = 1e-5
EPS_GN = 64e-5

VMEM_LIMIT_BYTES = 56 * 1024 * 1024

WKV_CHUNK = 64
WKV_HALF = 128
CONV_HALO = 16
IN_HALO = 16
GATE_CHUNK = 512
SUBLANES = 8
DFT_BLK = 256
NYQ_ROWS = 16


def _params(*sem):
    return pltpu.CompilerParams(dimension_semantics=sem, vmem_limit_bytes=VMEM_LIMIT_BYTES)


def _tiles(n, tile):
    assert n % tile == 0, (n, tile)
    return n // tile


def _dot(a, b):
    return jnp.dot(a.astype(BF16), b.astype(BF16), preferred_element_type=F32)


def _dot_nt(a, b):
    return lax.dot_general(a.astype(BF16), b.astype(BF16), (((1,), (1,)), ((), ())),
                           preferred_element_type=F32)


def _split_dot(x, m):
    xh = x.astype(BF16)
    xl = (x - xh.astype(F32)).astype(BF16)
    return jnp.dot(xh, m, preferred_element_type=F32) + jnp.dot(xl, m, preferred_element_type=F32)


def _split_dot_lhs(m, x):
    xh = x.astype(BF16)
    xl = (x - xh.astype(F32)).astype(BF16)
    return jnp.dot(m, xh, preferred_element_type=F32) + jnp.dot(m, xl, preferred_element_type=F32)


def _rms(x, g):
    return x * lax.rsqrt(jnp.mean(x * x, axis=-1, keepdims=True) + EPS_RMS) * g


def _in_proj_kernel(x_ref, xp_ref, xn_ref, g_ref, w_ref, gb_ref, mu_ref, w0_ref, wup_ref, a0_ref, aup_ref,
                    gup_ref, kk_ref, ka_ref, rk_ref, bd_ref,
                    u_ref, zc_ref, gt_ref, r_ref, v_ref, na_ref, g_out_ref, bon_ref, lw_ref, kd_ref, b_ref,
                    za_ref):
    i = pl.program_id(1)
    n = pl.num_programs(1)
    tm = x_ref.shape[1]
    hn = _rms(x_ref[0], g_ref[...]).astype(BF16)
    hp = jnp.where(i > 0, _rms(xp_ref[0], g_ref[...]), 0.0).astype(BF16)
    hx = jnp.where(i < n - 1, _rms(xn_ref[0], g_ref[...]), 0.0).astype(BF16)
    proj = lambda h, lo, width: jnp.dot(h, w_ref[:, lo:lo + width], preferred_element_type=F32)

    gate_lo = N_A_IN + 2 * D_B + D_C

    def gate_chunk(j):
        cols = slice(j * GATE_CHUNK, (j + 1) * GATE_CHUNK)
        gt_ref[0, :, cols] = jax.nn.sigmoid(
            proj(hn, gate_lo + j * GATE_CHUNK, GATE_CHUNK) + gb_ref[:, cols]).astype(BF16)

    zb = proj(hn, N_A_IN, 2 * D_B)
    za_ref[...] = proj(jnp.concatenate([hp, hn, hx], axis=0), 0, N_A_IN)
    u_ref[0] = zb[:, :D_B] * jax.nn.sigmoid(zb[:, D_B:])

    gate_chunk(0)
    z = za_ref[IN_HALO:IN_HALO + tm, :]
    zprev = za_ref[IN_HALO - 1:IN_HALO - 1 + tm, :]
    znext = za_ref[IN_HALO + 1:IN_HALO + 1 + tm, :]
    zs = mu_ref[0:1, :] * z + mu_ref[1:2, :] * zprev + mu_ref[2:3, :] * znext
    r = zs[:, 0:D_A]
    k = zs[:, D_A:2 * D_A]
    v = zs[:, 2 * D_A:3 * D_A]
    o = 3 * D_A
    tdw = jnp.tanh(zs[:, o:o + 2 * LORA]).astype(BF16)
    dab = zs[:, o + 2 * LORA:o + 4 * LORA].astype(BF16)
    dg = zs[:, o + 4 * LORA:o + 4 * LORA + LORA_G]
    bd = bd_ref[...]
    r_ref[0] = r
    v_ref[0] = v

    gate_chunk(1)
    g_out_ref[0] = _dot(jax.nn.sigmoid(dg), gup_ref[...])
    kk0 = k * kk_ref[...]
    nrm = jnp.sqrt(_dot(kk0 * kk0, bd))
    kk = kk0 / jnp.maximum(nrm, 1e-12)
    na_ref[0] = -kk
    ksum = jnp.zeros_like(k)
    for d in range(2):
        gate_chunk(2 + 2 * d)
        wl = w0_ref[d:d + 1, :] + jnp.dot(tdw, wup_ref[d], preferred_element_type=F32)
        lw_ref[0, d] = (-math.exp(-0.5)) * jax.nn.sigmoid(wl)
        a = jax.nn.sigmoid(a0_ref[d:d + 1, :] + jnp.dot(dab, aup_ref[d], preferred_element_type=F32))
        gate_chunk(3 + 2 * d)
        kd = k * (1.0 + (a - 1.0) * ka_ref[...])
        kd_ref[0, d] = kd
        b_ref[0, d] = kk * a
        ksum = ksum + kd
    zc_ref[0] = proj(hn, N_A_IN + 2 * D_B, D_C).astype(BF16)
    bon_ref[0] = _dot(r * rk_ref[...] * ksum, bd) * v


def _in_proj(x, lw, tm=256):
    b, s, _ = x.shape
    nh = tm // IN_HALO
    last = s // IN_HALO - 1
    const = lambda a: pl.BlockSpec(a.shape, lambda bi, i: (0,) * a.ndim, pipeline_mode=pl.Buffered(1))
    consts = (lw["norm_mix_g"], lw["w_in"], lw["gate_b"], lw["mu"], lw["w0"], lw["wup"], lw["a0"], lw["aup"],
              lw["gup"], lw["k_k"], lw["k_a"], lw["r_k"], lw["bd_ones"])
    tok = lambda width: pl.BlockSpec((1, tm, width), lambda bi, i: (bi, i, 0))
    dirs = pl.BlockSpec((1, 2, tm, D_A), lambda bi, i: (bi, 0, i, 0))
    tok_shape = lambda width, dt: jax.ShapeDtypeStruct((b, s, width), dt)
    dir_shape = jax.ShapeDtypeStruct((b, 2, s, D_A), F32)
    return pl.pallas_call(
        _in_proj_kernel,
        grid=(b, _tiles(s, tm)),
        in_specs=[tok(D_MODEL),
                  pl.BlockSpec((1, IN_HALO, D_MODEL), lambda bi, i: (bi, jnp.maximum(i * nh - 1, 0), 0)),
                  pl.BlockSpec((1, IN_HALO, D_MODEL), lambda bi, i: (bi, jnp.minimum((i + 1) * nh, last), 0)),
                  ] + [const(c) for c in consts],
        out_specs=[tok(D_B), tok(D_C), tok(3 * D_MODEL)] + [tok(D_A)] * 5 + [dirs] * 3,
        out_shape=[tok_shape(D_B, F32), tok_shape(D_C, BF16), tok_shape(3 * D_MODEL, BF16)]
        + [tok_shape(D_A, F32)] * 5 + [dir_shape] * 3,
        scratch_shapes=[pltpu.VMEM((tm + 2 * IN_HALO, N_A_IN), F32)],
        compiler_params=_params("parallel", "parallel"),
        name="in_proj",
    )(x, x, x, *consts)


def _wkv_masks(reverse):
    C, HW = WKV_CHUNK, WKV_HALF
    row = lax.broadcasted_iota(jnp.int32, (C, HW), 0)
    lane_i = lax.broadcasted_iota(jnp.int32, (C, HW), 1) & (C - 1)
    dt = (lane_i - row) if reverse else (row - lane_i)
    ti = lax.broadcasted_iota(jnp.int32, (C, C), 0)
    tj = lax.broadcasted_iota(jnp.int32, (C, C), 1)
    rr = lax.broadcasted_iota(jnp.int32, (HW, HW), 0)
    cc = lax.broadcasted_iota(jnp.int32, (HW, HW), 1)
    return dict(tri=jnp.where((tj >= ti) if reverse else (tj <= ti), 1.0, 0.0).astype(BF16),
                strict=dt > 0, incl=dt >= 0, eye=dt == 0,
                bd=(rr >> 6) == (cc >> 6), diag=rr == cc)


def _wkv_chunks(chains):
    C, HW = WKV_CHUNK, WKV_HALF
    n = range(len(chains))
    r, a, b, k, v, lw, sbd, m, rev = zip(*chains)
    bd = lambda i, x: jnp.where(m[i]["bd"], jnp.concatenate([x] * (HW // C), axis=0), 0.0)

    cum = [_split_dot_lhs(m[i]["tri"], lw[i]) for i in n]
    tot = [cum[i][0:1, :] if rev[i] else cum[i][C - 1:C, :] for i in n]
    e_inv = [jnp.exp(-cum[i]) for i in n]
    e_rem = [jnp.exp(tot[i] - cum[i]) for i in n]
    at = [a[i] * jnp.exp(cum[i] - lw[i]) for i in n]
    rt = [r[i] * jnp.exp(cum[i]) for i in n]

    gram = [_dot_nt(jnp.concatenate([at[i], rt[i]], axis=0),
                    jnp.concatenate([bd(i, b[i] * e_inv[i]), bd(i, k[i] * e_inv[i])], axis=0)) for i in n]
    l_ab = [jnp.where(m[i]["strict"], gram[i][:C, :HW], 0.0) for i in n]
    l_ak = [jnp.where(m[i]["strict"], gram[i][:C, HW:], 0.0) for i in n]
    m_rb = [jnp.where(m[i]["incl"], gram[i][C:, :HW], 0.0) for i in n]
    m_rk = [jnp.where(m[i]["incl"], gram[i][C:, HW:], 0.0) for i in n]

    t_inv = [jnp.where(m[i]["eye"], 1.0, 0.0) + l_ab[i] for i in n]
    lp = [_dot(l_ab[i], bd(i, l_ab[i])) for i in n]
    nsq = int(math.log2(C)) - 1
    for j in range(1, nsq):
        res = [_dot(jnp.concatenate([lp[i], t_inv[i]], axis=0), bd(i, lp[i])) for i in n]
        lp = [res[i][:C] for i in n]
        t_inv = [t_inv[i] + res[i][C:] for i in n]
    t_inv = [t_inv[i] + _dot(t_inv[i], bd(i, lp[i])) for i in n]

    lv = [_dot(jnp.concatenate([l_ak[i], m_rk[i]], axis=0), bd(i, v[i])) for i in n]
    pq = [_dot(t_inv[i], jnp.concatenate([bd(i, at[i]), bd(i, lv[i][:C])], axis=1)) for i in n]
    mpq = [_dot(m_rb[i], jnp.concatenate([bd(i, pq[i][:, :HW]), bd(i, pq[i][:, HW:])], axis=1)) for i in n]
    rp = [rt[i] + mpq[i][:, :HW] for i in n]
    yq = [mpq[i][:, HW:] + lv[i][C:] for i in n]

    o1 = [_dot(jnp.concatenate([pq[i][:, :HW], rp[i]], axis=0), sbd[i]) for i in n]
    lhs_t = [jnp.transpose(jnp.concatenate([b[i] * e_rem[i], k[i] * e_rem[i]], axis=0)) for i in n]
    upd = [_dot(lhs_t[i], jnp.concatenate([o1[i][:C] + pq[i][:, HW:], v[i]], axis=0)) for i in n]
    wc_col = [jnp.sum(jnp.where(m[i]["diag"], jnp.exp(tot[i]), 0.0), axis=1, keepdims=True) for i in n]
    return [(o1[i][C:] + yq[i], wc_col[i] * sbd[i] + jnp.where(m[i]["bd"], upd[i], 0.0)) for i in n]


def _wkv_kernel(rf_ref, vf_ref, af_ref, lwf_ref, kdf_ref, bf_ref,
                rb_ref, vb_ref, ab_ref, lwb_ref, kdb_ref, bb_ref, yf_ref, yb_ref, s_ref, *, nb):
    @pl.when(pl.program_id(1) == 0)
    def _():
        s_ref[...] = jnp.zeros_like(s_ref)

    dirs = ((rf_ref, vf_ref, af_ref, lwf_ref, kdf_ref, bf_ref, yf_ref),
            (rb_ref, vb_ref, ab_ref, lwb_ref, kdb_ref, bb_ref, yb_ref))
    masks = (_wkv_masks(False), _wkv_masks(True))
    chains, dests = [], []
    for bi in range(nb):
        for d, (r_ref, v_ref, a_ref, lw_ref, kd_ref, b_ref, y_ref) in enumerate(dirs):
            for h in range(D_A // WKV_HALF):
                sl = slice(h * WKV_HALF, (h + 1) * WKV_HALF)
                chains.append((r_ref[bi, :, sl], a_ref[bi, :, sl], b_ref[bi, 0, :, sl], kd_ref[bi, 0, :, sl],
                               v_ref[bi, :, sl], lw_ref[bi, 0, :, sl], s_ref[bi, d, h], masks[d], d == 1))
                dests.append((y_ref, bi, sl, d, h))
    for (y, s_new), (y_ref, bi, sl, d, h) in zip(_wkv_chunks(chains), dests):
        y_ref[bi, :, sl] = y
        s_ref[bi, d, h] = s_new


def _wkv(r, v, na, lw, kd, bb, nb=4):
    b, s, _ = r.shape
    nc = _tiles(s, WKV_CHUNK)
    tok_f = pl.BlockSpec((nb, WKV_CHUNK, D_A), lambda bi, c: (bi, c, 0))
    tok_b = pl.BlockSpec((nb, WKV_CHUNK, D_A), lambda bi, c: (bi, nc - 1 - c, 0))
    dir_f = pl.BlockSpec((nb, 1, WKV_CHUNK, D_A), lambda bi, c: (bi, 0, c, 0))
    dir_b = pl.BlockSpec((nb, 1, WKV_CHUNK, D_A), lambda bi, c: (bi, 1, nc - 1 - c, 0))
    y_shape = jax.ShapeDtypeStruct((b, s, D_A), F32)
    return pl.pallas_call(
        functools.partial(_wkv_kernel, nb=nb),
        grid=(_tiles(b, nb), nc),
        in_specs=[tok_f] * 3 + [dir_f] * 3 + [tok_b] * 3 + [dir_b] * 3,
        out_specs=[tok_f, tok_b],
        out_shape=[y_shape, y_shape],
        scratch_shapes=[pltpu.VMEM((nb, 2, D_A // WKV_HALF, WKV_HALF, WKV_HALF), F32)],
        compiler_params=_params("parallel", "arbitrary"),
        name="wkv_scan",
    )(r, v, na, lw, kd, bb, r, v, na, lw, kd, bb)


def _dft_kernel(x_ref, cw_ref, rev_ref, dm_ref, o_ref, rhs_ref, nyq_ref):
    i = pl.program_id(1)
    s = x_ref.shape[1]
    tm = o_ref.shape[1]
    half = s // 2
    nblk = s // DFT_BLK
    groups = [slice(g * GW_C, (g + 1) * GW_C) for g in range(D_C // GW_C)]

    @pl.when(i == 0)
    def _():
        for blk in range(half // DFT_BLK):
            rows = slice(blk * DFT_BLK, (blk + 1) * DFT_BLK)
            hi = x_ref[0, (nblk - blk - 1) * DFT_BLK:(nblk - blk) * DFT_BLK, :]
            if blk == 0:
                xr = jnp.dot(rev_ref[:, :DFT_BLK], hi, preferred_element_type=F32)
            else:
                top = x_ref[0, (nblk - blk) * DFT_BLK:(nblk - blk + 1) * DFT_BLK, :]
                xr = jnp.dot(rev_ref[...], jnp.concatenate([hi, top], axis=0), preferred_element_type=F32)
            xa = x_ref[0, rows, :].astype(F32)
            xe = (xa + xr).astype(BF16)
            xo = (xa - xr).astype(BF16)
            for sl in groups:
                rhs_ref[rows, sl] = jnp.dot(xe[:, sl], cw_ref[:, :GW_C], preferred_element_type=F32).astype(BF16)
                rhs_ref[half + blk * DFT_BLK:half + (blk + 1) * DFT_BLK, sl] = jnp.dot(
                    xo[:, sl], cw_ref[:, GW_C:], preferred_element_type=F32).astype(BF16)
        mid = x_ref[0, half:half + NYQ_ROWS, :]
        for sl in groups:
            nyq_ref[:, sl] = jnp.dot(mid[:, sl], cw_ref[:, :GW_C], preferred_element_type=F32) * s ** -0.5

    odd = (lax.broadcasted_iota(jnp.int32, (tm, D_C), 0) & 1) == 1
    nyq = nyq_ref[0:1, :]
    o_ref[0] = (jnp.dot(dm_ref[...], rhs_ref[...], preferred_element_type=F32)
                + jnp.where(odd, -nyq, nyq)).astype(o_ref.dtype)


def _dft(zc, cw, rev, dm, tm=1024):
    b, s, _ = zc.shape
    assert tm % 2 == 0 and (s // 2) % DFT_BLK == 0
    return pl.pallas_call(
        _dft_kernel,
        grid=(b, _tiles(s, tm)),
        in_specs=[pl.BlockSpec((1, s, D_C), lambda bi, i: (bi, 0, 0)),
                  pl.BlockSpec(cw.shape, lambda bi, i: (0, 0)),
                  pl.BlockSpec(rev.shape, lambda bi, i: (0, 0)),
                  pl.BlockSpec((tm, s), lambda bi, i: (i, 0))],
        out_specs=pl.BlockSpec((1, tm, D_C), lambda bi, i: (bi, i, 0)),
        out_shape=jax.ShapeDtypeStruct((b, s, D_C), BF16),
        scratch_shapes=[pltpu.VMEM((s, D_C), BF16), pltpu.VMEM((NYQ_ROWS, D_C), F32)],
        compiler_params=_params("parallel", "arbitrary"),
        name="dft2_real",
    )(zc, cw, rev, dm)


def _dft_tables(s):
    def cs(n, cols):
        ang = ((jnp.arange(n, dtype=jnp.int32)[:, None] * jnp.arange(cols, dtype=jnp.int32)[None, :]) % n
               ).astype(F32) * (2.0 * math.pi / n)
        return jnp.cos(ang), jnp.sin(ang)
    cc, sc = cs(GW_C, GW_C)
    cw = (jnp.concatenate([cc, sc], axis=1) * GW_C ** -0.5).astype(BF16)
    j = jnp.arange(DFT_BLK, dtype=jnp.int32)[:, None]
    c = jnp.arange(2 * DFT_BLK, dtype=jnp.int32)[None, :]
    rev = (((j >= 1) & (c == DFT_BLK - j)) | ((j == 0) & (c == DFT_BLK))).astype(BF16)
    cp, sp = cs(s, s // 2)
    dm = (jnp.concatenate([cp, -sp], axis=1) * s ** -0.5).astype(BF16)
    return cw, rev, dm


def _merge_kernel(x_ref, yf_ref, yb_ref, bon_ref, g_ref, u_ref, up_ref, un_ref, f_ref, gt_ref,
                  lnxg_ref, lnxb_ref, cw_ref, cb_ref, clng_ref, clnb_ref, bdm_ref,
                  wa_ref, wb_ref, wc_ref, wo_ref, o_ref, us_ref, c_ref, *, sub):
    i = pl.program_id(1)
    n = pl.num_programs(1)
    tm = x_ref.shape[1]
    rows = tm + 2 * CONV_HALO

    bdm = bdm_ref[...]
    ys = yf_ref[0] + yb_ref[0]
    dev = ys - _split_dot(ys, bdm)
    var = _dot(dev * dev, bdm)
    yn = dev * lax.rsqrt(var + EPS_GN) * lnxg_ref[...] + lnxb_ref[...]
    ya = _dot((yn + bon_ref[0]) * g_ref[0], wa_ref[...])
    yc = _dot(f_ref[0], wc_ref[...])

    us_ref[0, 0:CONV_HALO, :] = jnp.where(i > 0, up_ref[0], 0.0)
    us_ref[0, CONV_HALO:CONV_HALO + tm, :] = u_ref[0]
    us_ref[0, CONV_HALO + tm:rows, :] = jnp.where(i < n - 1, un_ref[0], 0.0)
    for s in range(1, SUBLANES):
        us_ref[s, 0:rows - SUBLANES, :] = us_ref[0, s:s + rows - SUBLANES, :]
    first = CONV_HALO - CONV_K // 2
    for t0 in range(0, tm, sub):
        acc = jnp.zeros((sub, D_B), F32) + cb_ref[...]
        for j in range(CONV_K):
            q, s = divmod(first + j, SUBLANES)
            w = jnp.concatenate([cw_ref[j * SUBLANES:(j + 1) * SUBLANES, :]] * (sub // SUBLANES), axis=0)
            acc = acc + w * us_ref[s, t0 + q * SUBLANES:t0 + q * SUBLANES + sub, :]
        c_ref[t0:t0 + sub, :] = acc
    c = c_ref[...]
    cd = c - jnp.mean(c, axis=-1, keepdims=True)
    cn = cd * lax.rsqrt(jnp.mean(cd * cd, axis=-1, keepdims=True) + EPS_LN) * clng_ref[...] + clnb_ref[...]
    yb = _dot(cn * jax.nn.sigmoid(cn), wb_ref[...])

    gate = lambda j: gt_ref[0, :, j * D_MODEL:(j + 1) * D_MODEL].astype(F32)
    merged = gate(0) * ya + gate(1) * yb + gate(2) * yc
    o_ref[0] = x_ref[0] + _dot(merged, wo_ref[...])


def _merge(x, yf, yb, bon, g, u, f, gates, lw, tm=256, sub=32):
    b, s, _ = x.shape
    nh = tm // CONV_HALO
    last = s // CONV_HALO - 1
    const = lambda a: pl.BlockSpec(a.shape, lambda bi, i: (0,) * a.ndim, pipeline_mode=pl.Buffered(1))
    tok = lambda n: pl.BlockSpec((1, tm, n), lambda bi, i: (bi, i, 0))
    consts = (lw["lnx_g"], lw["lnx_b"], lw["conv_w"], lw["conv_b"], lw["conv_ln_g"], lw["conv_ln_b"],
              lw["bd_mean"], lw["w_a_out"], lw["w_b_out"], lw["w_c_out"], lw["w_out"])
    return pl.pallas_call(
        functools.partial(_merge_kernel, sub=sub),
        grid=(b, _tiles(s, tm)),
        in_specs=[tok(D_MODEL), tok(D_A), tok(D_A), tok(D_A), tok(D_A), tok(D_B),
                  pl.BlockSpec((1, CONV_HALO, D_B), lambda bi, i: (bi, jnp.maximum(i * nh - 1, 0), 0)),
                  pl.BlockSpec((1, CONV_HALO, D_B), lambda bi, i: (bi, jnp.minimum((i + 1) * nh, last), 0)),
                  tok(D_C), tok(3 * D_MODEL)] + [const(a) for a in consts],
        out_specs=tok(D_MODEL),
        out_shape=jax.ShapeDtypeStruct(x.shape, F32),
        scratch_shapes=[pltpu.VMEM((SUBLANES, tm + 2 * CONV_HALO, D_B), F32), pltpu.VMEM((tm, D_B), F32)],
        compiler_params=_params("parallel", "parallel"),
        name="merge",
    )(x, yf, yb, bon, g, u, u, u, f, gates, *consts)


def _norm_proj_kernel(x_ref, g_ref, w_ref, o_ref):
    o_ref[...] = _dot(_rms(x_ref[...], g_ref[...]), w_ref[...]).astype(o_ref.dtype)


def _norm_proj(x2, g, w, out_dtype, tm=256):
    t = x2.shape[0]
    n = w.shape[1]
    return pl.pallas_call(
        _norm_proj_kernel,
        grid=(_tiles(t, tm),),
        in_specs=[pl.BlockSpec((tm, D_MODEL), lambda i: (i, 0)),
                  pl.BlockSpec((1, D_MODEL), lambda i: (0, 0)),
                  pl.BlockSpec(w.shape, lambda i: (0, 0))],
        out_specs=pl.BlockSpec((tm, n), lambda i: (i, 0)),
        out_shape=jax.ShapeDtypeStruct((t, n), out_dtype),
        compiler_params=_params("parallel"),
        name="norm_proj",
    )(x2, g, w)


def _xattn_kernel(x_ref, kv_ref, g_ref, wq_ref, wo_ref, o_ref):
    x = x_ref[0]
    q = (_dot(_rms(x, g_ref[...]), wq_ref[...]) * HD_X ** -0.5).astype(BF16)
    heads = []
    for h in range(H_X):
        sl = slice(h * HD_X, (h + 1) * HD_X)
        sc = _dot_nt(q[:, sl], kv_ref[0, :, sl])
        e = jnp.exp(sc - jnp.max(sc, axis=-1, keepdims=True))
        p = e / jnp.sum(e, axis=-1, keepdims=True)
        heads.append(_dot(p, kv_ref[0, :, D_MODEL + h * HD_X:D_MODEL + (h + 1) * HD_X]))
    o_ref[0] = x + _dot(jnp.concatenate(heads, axis=1), wo_ref[...])


def _xattn(x, kv, lw, tm=1024):
    b, s, _ = x.shape
    full = lambda a: pl.BlockSpec(a.shape, lambda bi, i: (0,) * a.ndim)
    return pl.pallas_call(
        _xattn_kernel,
        grid=(b, _tiles(s, tm)),
        in_specs=[pl.BlockSpec((1, tm, D_MODEL), lambda bi, i: (bi, i, 0)),
                  pl.BlockSpec((1, N_MEM, 2 * D_MODEL), lambda bi, i: (bi, 0, 0)),
                  full(lw["norm_x_g"]), full(lw["wq"]), full(lw["wo"])],
        out_specs=pl.BlockSpec((1, tm, D_MODEL), lambda bi, i: (bi, i, 0)),
        out_shape=jax.ShapeDtypeStruct(x.shape, F32),
        compiler_params=_params("parallel", "parallel"),
        name="cross_attn",
    )(x, kv, lw["norm_x_g"], lw["wq"], lw["wo"])


def _mlp_kernel(x_ref, g_ref, w1_ref, w2_ref, fg_ref, o_ref, hn_ref, acc_ref, *, final):
    j = pl.program_id(1)

    @pl.when(j == 0)
    def _():
        hn_ref[...] = _rms(x_ref[...], g_ref[...]).astype(BF16)
        acc_ref[...] = jnp.zeros_like(acc_ref)

    h1 = jnp.maximum(jnp.dot(hn_ref[...], w1_ref[...], preferred_element_type=F32), 0.0)
    acc_ref[...] += _dot(h1 * h1, w2_ref[...])

    @pl.when(j == pl.num_programs(1) - 1)
    def _():
        y = x_ref[...] + acc_ref[...]
        o_ref[...] = _rms(y, fg_ref[...]) if final else y


def _mlp(x2, lw, final_g, final, tm=1024, tf=2048):
    t = x2.shape[0]
    return pl.pallas_call(
        functools.partial(_mlp_kernel, final=final),
        grid=(_tiles(t, tm), _tiles(D_FF, tf)),
        in_specs=[pl.BlockSpec((tm, D_MODEL), lambda i, j: (i, 0)),
                  pl.BlockSpec((1, D_MODEL), lambda i, j: (0, 0)),
                  pl.BlockSpec((D_MODEL, tf), lambda i, j: (0, j)),
                  pl.BlockSpec((tf, D_MODEL), lambda i, j: (j, 0)),
                  pl.BlockSpec((1, D_MODEL), lambda i, j: (0, 0))],
        out_specs=pl.BlockSpec((tm, D_MODEL), lambda i, j: (i, 0)),
        out_shape=jax.ShapeDtypeStruct((t, D_MODEL), F32),
        scratch_shapes=[pltpu.VMEM((tm, D_MODEL), BF16), pltpu.VMEM((tm, D_MODEL), F32)],
        compiler_params=_params("parallel", "arbitrary"),
        name="mlp",
    )(x2, lw["norm_mlp_g"], lw["w_mlp1"], lw["w_mlp2"], final_g)


def _stage_layer(p, l):
    row = lambda a: a.reshape(1, -1).astype(F32)
    pad_lora = lambda w: jnp.stack([
        jnp.zeros((2 * LORA, D_A), F32).at[d * LORA:(d + 1) * LORA].set(w[d]) for d in range(2)]).astype(BF16)
    hid = jnp.arange(D_A, dtype=jnp.int32) // HEAD_A
    same_head = (hid[:, None] == hid[None, :]).astype(F32)
    return dict(
        norm_mix_g=row(p["norm_mix_g"][l]), w_in=p["w_in"][l].astype(BF16),
        mu=jnp.concatenate([1.0 - p["shift_mu"][l][0:1] - p["shift_mu"][l][1:2], p["shift_mu"][l]], axis=0),
        w0=p["w0"][l], wup=pad_lora(p["w_up"][l]), a0=p["a0"][l],
        aup=pad_lora(p["a_up"][l]), gup=p["g_up"][l].astype(BF16),
        k_k=row(p["k_k"][l]), k_a=row(p["k_a"][l]), r_k=row(p["r_k"][l]),
        lnx_g=row(p["lnx_g"][l]), lnx_b=row(p["lnx_b"][l]),
        bd_ones=same_head.astype(BF16), bd_mean=(same_head / HEAD_A).astype(BF16),
        conv_w=jnp.repeat(p["conv_w"][l], SUBLANES, axis=0), conv_b=row(p["conv_b"][l]),
        conv_ln_g=row(p["conv_ln_g"][l]), conv_ln_b=row(p["conv_ln_b"][l]),
        gate_b=row(p["gate_b"][l]),
        w_a_out=p["w_a_out"][l].astype(BF16), w_b_out=p["w_b_out"][l].astype(BF16),
        w_c_out=p["w_c_out"][l].astype(BF16), w_out=p["w_out"][l].astype(BF16),
        norm_x_g=row(p["norm_x_g"][l]), norm_mem_g=row(p["norm_mem_g"][l]),
        wq=p["wq"][l].astype(BF16), wo=p["wo"][l].astype(BF16),
        wkv=jnp.concatenate([p["wk"][l], p["wv"][l]], axis=1).astype(BF16),
        norm_mlp_g=row(p["norm_mlp_g"][l]),
        w_mlp1=p["w_mlp1"][l].astype(BF16), w_mlp2=p["w_mlp2"][l].astype(BF16),
    )


def _trunk(x, mem, layers, final_g):
    b, s, _ = x.shape
    t = b * s
    cw, rev, dm = _dft_tables(s)
    mem2 = mem.reshape(b * N_MEM, D_MODEL)
    for l, lw in enumerate(layers):
        u, zc, gates, r, v, na, g, bon, lwd, kd, bb = _in_proj(x, lw)
        yf, yb = _wkv(r, v, na, lwd, kd, bb)
        f = _dft(zc, cw, rev, dm)
        x = _merge(x, yf, yb, bon, g, u, f, gates, lw)
        kv = _norm_proj(mem2, lw["norm_mem_g"], lw["wkv"], BF16).reshape(b, N_MEM, 2 * D_MODEL)
        x = _xattn(x, kv, lw)
        x = _mlp(x.reshape(t, D_MODEL), lw, final_g, l == len(layers) - 1).reshape(b, s, D_MODEL)
    return x


def kernel(x_prompt, x_sample, mem_prompt, mem_sample, norm_mix_g, w_in, shift_mu, w0, w_up, a0, a_up, g_up, k_k, k_a, r_k, lnx_g, lnx_b, w_a_out, conv_w, conv_b, conv_ln_g, conv_ln_b, w_b_out, w_c_out, gate_b, w_out, norm_x_g, norm_mem_g, wq, wk, wv, wo, norm_mlp_g, w_mlp1, w_mlp2, final_norm_g):
    p = dict(norm_mix_g=norm_mix_g, w_in=w_in, shift_mu=shift_mu, w0=w0, w_up=w_up, a0=a0,
             a_up=a_up, g_up=g_up, k_k=k_k, k_a=k_a, r_k=r_k, lnx_g=lnx_g, lnx_b=lnx_b,
             w_a_out=w_a_out, conv_w=conv_w, conv_b=conv_b, conv_ln_g=conv_ln_g,
             conv_ln_b=conv_ln_b, w_b_out=w_b_out, w_c_out=w_c_out, gate_b=gate_b, w_out=w_out,
             norm_x_g=norm_x_g, norm_mem_g=norm_mem_g, wq=wq, wk=wk, wv=wv, wo=wo,
             norm_mlp_g=norm_mlp_g, w_mlp1=w_mlp1, w_mlp2=w_mlp2)
    layers = [_stage_layer(p, l) for l in range(DEPTH)]
    final_g = final_norm_g.reshape(1, D_MODEL).astype(F32)
    return (_trunk(x_prompt, mem_prompt, layers, final_g), _trunk(x_sample, mem_sample, layers, final_g))
```
